```python
import math
import jax, jax.numpy as jnp
from jax import lax
import numpy as np

D_MODEL = 1024
BATCH = 4
SEQ = 4096
DEPTH = 2
DEC_BATCH = 128
DEC_SEQ = 8
PAST_LEN = 2048
PAGE_SIZE = 128

N_HEADS = 8
HEAD_DIM = D_MODEL // (2 * N_HEADS)
V_DIM = 2 * HEAD_DIM
ATTN_WIDTH = N_HEADS * V_DIM
QK_COLS = N_HEADS * 2 * HEAD_DIM
Q_BLOCK = 128
CHUNK = 128
GMLP_WIDTH = D_MODEL
N_GROUPS = 8
GROUP_DIM = GMLP_WIDTH // N_GROUPS
D_FF = 2816
N_EXPERTS = 8
TOP_K = 2
D_FF_EXPERT = 3584
N_DENSE = (DEPTH + 1) // 2
N_MOE = DEPTH // 2
EPS = 1e-6
NEG = -1e30
SPLITS = (QK_COLS, 2 * QK_COLS, 2 * QK_COLS + ATTN_WIDTH,
          2 * QK_COLS + ATTN_WIDTH + 2 * GMLP_WIDTH,
          2 * QK_COLS + ATTN_WIDTH + 2 * GMLP_WIDTH + ATTN_WIDTH)
IN_COLS = SPLITS[-1] + GMLP_WIDTH

kernel_name = "hybrid_diffattn_chunk_sgu_moe_step"


def rmsnorm(x, g):
    xf = x.astype(jnp.float32)
    y = xf * lax.rsqrt(jnp.mean(xf * xf, -1, keepdims=True) + EPS)
    return (y * g.astype(jnp.float32)).astype(x.dtype)


def layernorm(x, g, b):
    xf = x.astype(jnp.float32)
    mu = jnp.mean(xf, -1, keepdims=True)
    xc = xf - mu
    y = xc * lax.rsqrt(jnp.mean(xc * xc, -1, keepdims=True) + EPS)
    return (y * g.astype(jnp.float32) + b.astype(jnp.float32)).astype(x.dtype)


def project(x, l, norm_mix, w_in):
    h = rmsnorm(x, norm_mix[l])
    z = h @ w_in[l]
    q, k, v, zg, ga, gs = jnp.split(z, SPLITS, axis=-1)
    n, t = x.shape[0], x.shape[1]
    q = q.reshape(n, t, N_HEADS, 2, HEAD_DIM)
    k = k.reshape(n, t, N_HEADS, 2, HEAD_DIM)
    v = v.reshape(n, t, N_HEADS, V_DIM)
    u, vg = jnp.split(jax.nn.gelu(zg), 2, axis=-1)
    return q, k, v, u, vg, ga, gs


def diff_lambda(l, lam_q1, lam_k1, lam_q2, lam_k2):
    lam_init = 0.8 - 0.6 * math.exp(-0.3 * l)
    lam = (jnp.exp(jnp.sum(lam_q1[l] * lam_k1[l]).astype(jnp.float32))
           - jnp.exp(jnp.sum(lam_q2[l] * lam_k2[l]).astype(jnp.float32)) + lam_init)
    return lam, lam_init


def diff_attn_prompt(q, k, v, lam):
    b, t = q.shape[0], q.shape[1]
    nb = t // Q_BLOCK
    scale = HEAD_DIM ** -0.5
    qb = jnp.moveaxis(q.reshape(b, nb, Q_BLOCK, N_HEADS, 2, HEAD_DIM), 1, 0)
    kpos = jnp.arange(t)

    def block(args):
        qi, i = args
        s = jnp.einsum('bqhcd,bkhcd->bhcqk', qi, k).astype(jnp.float32) * scale
        qpos = i * Q_BLOCK + jnp.arange(Q_BLOCK)
        mask = kpos[None, :] <= qpos[:, None]
        p = jax.nn.softmax(jnp.where(mask, s, NEG), axis=-1)
        a = p[:, :, 0] - lam * p[:, :, 1]
        return jnp.einsum('bhqk,bkhe->bqhe', a.astype(v.dtype), v)

    o = lax.map(block, (qb, jnp.arange(nb)))
    return jnp.moveaxis(o, 0, 1).reshape(b, t, N_HEADS, V_DIM)


def diff_attn_sample(q, k_new, v_new, k_past, v_past, lam):
    s_len = q.shape[1]
    p_len = k_past.shape[1]
    scale = HEAD_DIM ** -0.5
    s_past = jnp.einsum('nqhcd,nkhcd->nhcqk', q, k_past).astype(jnp.float32) * scale
    s_new = jnp.einsum('nqhcd,nkhcd->nhcqk', q, k_new).astype(jnp.float32) * scale
    causal = jnp.tril(jnp.ones((s_len, s_len), dtype=bool))
    s_new = jnp.where(causal, s_new, NEG)
    p = jax.nn.softmax(jnp.concatenate([s_past, s_new], axis=-1), axis=-1)
    a = (p[:, :, 0] - lam * p[:, :, 1]).astype(v_new.dtype)
    return (jnp.einsum('nhqk,nkhe->nqhe', a[..., :p_len], v_past)
            + jnp.einsum('nhqk,nkhe->nqhe', a[..., p_len:], v_new))


def head_out(o, l, attn_head_gain, lam_init):
    o = rmsnorm(o, attn_head_gain[l]) * (1.0 - lam_init)
    return o.reshape(o.shape[0], o.shape[1], ATTN_WIDTH)


def sgu_prompt(u, vg, l, sgu_ln_g, sgu_ln_b, sgu_w, sgu_b):
    b, t, _ = vg.shape
    vn = layernorm(vg, sgu_ln_g[l], sgu_ln_b[l])
    vc = vn.reshape(b, t // CHUNK, CHUNK, N_GROUPS, GROUP_DIM)
    w = jnp.where(jnp.tril(jnp.ones((CHUNK, CHUNK), dtype=bool)), sgu_w[l], 0.0)
    s = jnp.einsum('gts,bcsgk->bctgk', w.astype(vc.dtype), vc) + sgu_b[l].T[:, :, None]
    return u * s.reshape(b, t, GMLP_WIDTH)


def sgu_sample(u, vg, l, sgu_ln_g, sgu_ln_b, sgu_w, sgu_b):
    n, s_len, _ = vg.shape
    vn = layernorm(vg, sgu_ln_g[l], sgu_ln_b[l])
    vc = vn.reshape(n, s_len, N_GROUPS, GROUP_DIM)
    w = jnp.where(jnp.tril(jnp.ones((CHUNK, CHUNK), dtype=bool)), sgu_w[l], 0.0)[:, :s_len, :s_len]
    s = jnp.einsum('gts,nsgk->ntgk', w.astype(vc.dtype), vc) + sgu_b[l][:, :s_len].T[:, :, None]
    return u * s.reshape(n, s_len, GMLP_WIDTH), vn


def swiglu(h, wg, wu, wd):
    return (jax.nn.silu(h @ wg) * (h @ wu)) @ wd


def moe(h, router, wg, wu, wd):
    logits = (h @ router).astype(jnp.float32)
    top_v, top_i = lax.top_k(logits, TOP_K)
    gate = jax.nn.softmax(top_v, axis=-1)
    combine = jnp.sum(jax.nn.one_hot(top_i, N_EXPERTS, dtype=jnp.float32) * gate[..., None], axis=-2)
    out = jnp.zeros_like(h)
    for e in range(N_EXPERTS):
        out = out + combine[..., e:e + 1].astype(h.dtype) * swiglu(h, wg[e], wu[e], wd[e])
    return out


def channel(x, l, norm_ffn, ffn_w_gate, ffn_w_up, ffn_w_down,
            moe_router, moe_w_gate, moe_w_up, moe_w_down):
    h = rmsnorm(x, norm_ffn[l])
    j = l // 2
    if l % 2 == 0:
        f = swiglu(h, ffn_w_gate[j], ffn_w_up[j], ffn_w_down[j])
    else:
        f = moe(h, moe_router[j], moe_w_gate[j], moe_w_up[j], moe_w_down[j])
    return x + f


def setup_inputs(seed: int = 0) -> dict:
    key = jax.random.key(seed)
    ks = jax.random.split(key, 32)
    n_pages = PAST_LEN // PAGE_SIZE
    n_used = DEC_BATCH * n_pages
    n_pool = n_used + n_used // 4

    def nrm(k, shape, scale):
        return jax.random.normal(k, shape, jnp.float32) * scale

    return {
        "x_prompt": nrm(ks[0], (BATCH, SEQ, D_MODEL), 1.0),
        "x_sample": nrm(ks[1], (DEC_BATCH, DEC_SEQ, D_MODEL), 1.0),
        "cache_k": nrm(ks[2], (DEPTH, n_pool, PAGE_SIZE, N_HEADS, 2, HEAD_DIM), 1.0),
        "cache_v": nrm(ks[3], (DEPTH, n_pool, PAGE_SIZE, N_HEADS, V_DIM), 1.0),
        "page_table": jax.random.permutation(ks[4], n_pool)[:n_used].reshape(DEC_BATCH, n_pages).astype(jnp.int32),
        "norm_mix": 1.0 + nrm(ks[5], (DEPTH, D_MODEL), 0.02),
        "w_in": nrm(ks[6], (DEPTH, D_MODEL, IN_COLS), D_MODEL ** -0.5),
        "lam_q1": nrm(ks[7], (DEPTH, HEAD_DIM), 0.1),
        "lam_k1": nrm(ks[8], (DEPTH, HEAD_DIM), 0.1),
        "lam_q2": nrm(ks[9], (DEPTH, HEAD_DIM), 0.1),
        "lam_k2": nrm(ks[10], (DEPTH, HEAD_DIM), 0.1),
        "attn_head_gain": 1.0 + nrm(ks[11], (DEPTH, N_HEADS, V_DIM), 0.02),
        "sgu_ln_g": 1.0 + nrm(ks[12], (DEPTH, GMLP_WIDTH), 0.02),
        "sgu_ln_b": nrm(ks[13], (DEPTH, GMLP_WIDTH), 0.02),
        "sgu_w": nrm(ks[14], (DEPTH, N_GROUPS, CHUNK, CHUNK), CHUNK ** -0.5),
        "sgu_b": 1.0 + nrm(ks[15], (DEPTH, N_GROUPS, CHUNK), 0.02),
        "w_out": nrm(ks[16], (DEPTH, D_MODEL, D_MODEL), D_MODEL ** -0.5),
        "norm_ffn": 1.0 + nrm(ks[17], (DEPTH, D_MODEL), 0.02),
        "ffn_w_gate": nrm(ks[18], (N_DENSE, D_MODEL, D_FF), D_MODEL ** -0.5),
        "ffn_w_up": nrm(ks[19], (N_DENSE, D_MODEL, D_FF), D_MODEL ** -0.5),
        "ffn_w_down": nrm(ks[20], (N_DENSE, D_FF, D_MODEL), D_FF ** -0.5),
        "moe_router": nrm(ks[21], (N_MOE, D_MODEL, N_EXPERTS), D_MODEL ** -0.5),
        "moe_w_gate": nrm(ks[22], (N_MOE, N_EXPERTS, D_MODEL, D_FF_EXPERT), D_MODEL ** -0.5),
        "moe_w_up": nrm(ks[23], (N_MOE, N_EXPERTS, D_MODEL, D_FF_EXPERT), D_MODEL ** -0.5),
        "moe_w_down": nrm(ks[24], (N_MOE, N_EXPERTS, D_FF_EXPERT, D_MODEL), D_FF_EXPERT ** -0.5),
        "norm_final": 1.0 + nrm(ks[25], (D_MODEL,), 0.02),
    }


def reference(x_prompt, x_sample, cache_k, cache_v, page_table, norm_mix, w_in,
              lam_q1, lam_k1, lam_q2, lam_k2, attn_head_gain, sgu_ln_g, sgu_ln_b,
              sgu_w, sgu_b, w_out, norm_ffn, ffn_w_gate, ffn_w_up, ffn_w_down,
              moe_router, moe_w_gate, moe_w_up, moe_w_down, norm_final):
    n_pages = PAST_LEN // PAGE_SIZE
    xp, xs = x_prompt, x_sample
    kp_l, vp_l, ks_l, vs_l, sv_l = [], [], [], [], []
    for l in range(DEPTH):
        lam, lam_init = diff_lambda(l, lam_q1, lam_k1, lam_q2, lam_k2)

        q, k, v, u, vg, ga, gs = project(xp, l, norm_mix, w_in)
        attn = head_out(diff_attn_prompt(q, k, v, lam), l, attn_head_gain, lam_init)
        sgu = sgu_prompt(u, vg, l, sgu_ln_g, sgu_ln_b, sgu_w, sgu_b)
        xp = xp + (jax.nn.sigmoid(ga) * attn + jax.nn.sigmoid(gs) * sgu) @ w_out[l]
        kp_l.append(k.reshape(k.shape[0], k.shape[1] // PAGE_SIZE, PAGE_SIZE, N_HEADS, 2, HEAD_DIM))
        vp_l.append(v.reshape(v.shape[0], v.shape[1] // PAGE_SIZE, PAGE_SIZE, N_HEADS, V_DIM))

        q, k, v, u, vg, ga, gs = project(xs, l, norm_mix, w_in)
        k_past = cache_k[l, page_table].reshape(xs.shape[0], n_pages * PAGE_SIZE, N_HEADS, 2, HEAD_DIM)
        v_past = cache_v[l, page_table].reshape(xs.shape[0], n_pages * PAGE_SIZE, N_HEADS, V_DIM)
        attn = head_out(diff_attn_sample(q, k, v, k_past, v_past, lam), l, attn_head_gain, lam_init)
        sgu, vn = sgu_sample(u, vg, l, sgu_ln_g, sgu_ln_b, sgu_w, sgu_b)
        xs = xs + (jax.nn.sigmoid(ga) * attn + jax.nn.sigmoid(gs) * sgu) @ w_out[l]
        ks_l.append(k)
        vs_l.append(v)
        sv_l.append(vn)

        xp = channel(xp, l, norm_ffn, ffn_w_gate, ffn_w_up, ffn_w_down,
                     moe_router, moe_w_gate, moe_w_up, moe_w_down)
        xs = channel(xs, l, norm_ffn, ffn_w_gate, ffn_w_up, ffn_w_down,
                     moe_router, moe_w_gate, moe_w_up, moe_w_down)

    y_prompt = rmsnorm(xp, norm_final)
    y_sample = rmsnorm(xs, norm_final)
    k_prompt = jnp.stack(kp_l)
    v_prompt = jnp.stack(vp_l)
    k_sample = jnp.stack(ks_l)
    v_sample = jnp.stack(vs_l)
    sgu_v_sample = jnp.stack(sv_l)
    return (y_prompt, y_sample, k_prompt, v_prompt, k_sample, v_sample, sgu_v_sample)
```

```python
import functools
import math

import jax
import jax.numpy as jnp
from jax import lax
from jax.experimental import pallas as pl
from jax.experimental.pallas import tpu as pltpu

F32 = jnp.float32
BF16 = jnp.bfloat16

N_HEADS = 8
HEAD_DIM = 64
V_DIM = 2 * HEAD_DIM
CHUNK = 128
N_GROUPS = 8
GROUP_DIM = 128
N_EXPERTS = 8
TOP_K = 2
PAGE_SIZE = 128
EPS = 1e-6
NEG = -1e30
QK_SCALE = HEAD_DIM ** -0.5
N_SEG = 7
LANES = 128
VMEM_LIMIT = 56 * 1024 * 1024

NT_DIMS = (((1,), (1,)), ((), ()))


def _cparams(n_axes, vmem=VMEM_LIMIT):
    return pltpu.CompilerParams(dimension_semantics=("arbitrary",) * n_axes,
                                vmem_limit_bytes=vmem)


def _rms(x, gain):
    return x * lax.rsqrt(jnp.mean(x * x, axis=-1, keepdims=True) + EPS) * gain


def _lam_value(lamp_ref, lam_init):
    a = jnp.sum(lamp_ref[0:1, :] * lamp_ref[1:2, :], axis=-1, keepdims=True)
    b = jnp.sum(lamp_ref[2:3, :] * lamp_ref[3:4, :], axis=-1, keepdims=True)
    return jnp.exp(a) - jnp.exp(b) + lam_init


def _proj_body(x_ref, g_ref, w_ref, lng_ref, lnb_ref,
               q_ref, k_ref, v_ref, u_ref, vn_ref, sga_ref, sgs_ref, extra_ref, *, paged):
    tm, d = x_ref.shape
    h = _rms(x_ref[...], g_ref[...]).astype(BF16)

    def seg(s):
        return jnp.dot(h, w_ref[:, s * d:(s + 1) * d], preferred_element_type=F32)

    q_ref[...] = (seg(0) * QK_SCALE).astype(BF16)
    if paged:
        kk = seg(1)
        for pg in range(tm // PAGE_SIZE):
            k_ref[pg] = kk[pg * PAGE_SIZE:(pg + 1) * PAGE_SIZE, :].T
        vv = seg(2)
        for hd in range(N_HEADS):
            v_ref[:, hd, :] = vv[:, hd * V_DIM:(hd + 1) * V_DIM]
        extra_ref[...] = vv.astype(BF16)
    else:
        k_ref[...] = seg(1)
        v_ref[...] = seg(2)
    u_ref[...] = jax.nn.gelu(seg(3)).astype(BF16)
    vg = jax.nn.gelu(seg(4))
    xc = vg - jnp.mean(vg, axis=-1, keepdims=True)
    vn = xc * lax.rsqrt(jnp.mean(xc * xc, axis=-1, keepdims=True) + EPS) * lng_ref[...] + lnb_ref[...]
    vn_ref[...] = vn.astype(BF16)
    if not paged:
        extra_ref[...] = vn
    sga_ref[...] = jax.nn.sigmoid(seg(5)).astype(BF16)
    sgs_ref[...] = jax.nn.sigmoid(seg(6)).astype(BF16)


def _proj(x, gain, w_bf, ln_g, ln_b, *, tm, paged):
    n, d = x.shape
    row = pl.BlockSpec((tm, d), lambda i: (i, 0))
    vec = pl.BlockSpec((1, d), lambda i: (0, 0))
    rows_bf = jax.ShapeDtypeStruct((n, d), BF16)
    rows_f32 = jax.ShapeDtypeStruct((n, d), F32)
    if paged:
        k_spec = pl.BlockSpec((tm // PAGE_SIZE, d, PAGE_SIZE), lambda i: (i, 0, 0))
        k_shape = jax.ShapeDtypeStruct((n // PAGE_SIZE, d, PAGE_SIZE), F32)
        v_spec = pl.BlockSpec((tm, N_HEADS, V_DIM), lambda i: (i, 0, 0))
        v_shape = jax.ShapeDtypeStruct((n, N_HEADS, V_DIM), F32)
    else:
        k_spec, k_shape, v_spec, v_shape = row, rows_f32, row, rows_f32
    out_specs = [row, k_spec, v_spec, row, row, row, row, row]
    out_shape = [rows_bf, k_shape, v_shape, rows_bf, rows_bf, rows_bf, rows_bf, rows_bf if paged else rows_f32]
    return pl.pallas_call(
        functools.partial(_proj_body, paged=paged),
        grid=(n // tm,),
        in_specs=[row, vec, pl.BlockSpec((d, N_SEG * d), lambda i: (0, 0)), vec, vec],
        out_specs=out_specs,
        out_shape=out_shape,
        compiler_params=_cparams(1),
        name="proj",
    )(x, gain.reshape(1, d), w_bf, ln_g.reshape(1, d), ln_b.reshape(1, d))


def _attn_prompt_body(q_ref, k_ref, vb, sga_ref, gain_ref, lamp_ref, o_ref,
                      kb, qs, acc, m_s, l_s, *, tq, lam_init):
    i = pl.program_id(2)

    @pl.when(i == 0)
    def _():
        for pg in range(k_ref.shape[0]):
            kb[:, pg * PAGE_SIZE:(pg + 1) * PAGE_SIZE] = k_ref[pg].astype(BF16)

    q = q_ref[...]
    lane = lax.broadcasted_iota(jnp.int32, q.shape, 1)
    zero = jnp.zeros_like(q)
    qs[0:tq, :] = jnp.where(lane < HEAD_DIM, q, zero)
    qs[tq:, :] = jnp.where(lane >= HEAD_DIM, q, zero)
    m_s[...] = jnp.full(m_s.shape, NEG, F32)
    l_s[...] = jnp.zeros(l_s.shape, F32)
    acc[...] = jnp.zeros(acc.shape, F32)

    def step(j, masked):
        start = pl.multiple_of(j * tq, tq)
        s = jnp.dot(qs[...], kb[:, pl.ds(start, tq)], preferred_element_type=F32)
        if masked:
            r = lax.broadcasted_iota(jnp.int32, s.shape, 0)
            r = jnp.where(r >= tq, r - tq, r)
            c = lax.broadcasted_iota(jnp.int32, s.shape, 1)
            s = jnp.where(c <= r, s, NEG)
        m_prev = m_s[...]
        m_new = jnp.maximum(m_prev, jnp.max(s, axis=-1, keepdims=True))
        alpha = jnp.exp(m_prev - m_new)
        p = jnp.exp(s - m_new)
        l_s[...] = alpha * l_s[...] + jnp.sum(p, axis=-1, keepdims=True)
        acc[...] = alpha * acc[...] + jnp.dot(p.astype(BF16), vb[pl.ds(start, tq), :],
                                              preferred_element_type=F32)
        m_s[...] = m_new

    def off_diag(j, carry):
        step(j, False)
        return carry

    lax.fori_loop(0, i, off_diag, 0)
    step(i, True)

    o = acc[...] / l_s[...]
    lam = _lam_value(lamp_ref, lam_init)
    od = o[0:tq, :] - lam * o[tq:, :]
    y = _rms(od, gain_ref[...]) * (1.0 - lam_init)
    o_ref[...] = (sga_ref[...].astype(F32) * y).astype(BF16)


def _attn_prompt(q, kt, v, sga, gain, lamp, *, lam_init, tq):
    b, t, d = q.shape
    qspec = pl.BlockSpec((None, tq, V_DIM), lambda bi, h, i: (bi, i, h))
    kspec = pl.BlockSpec((None, t // PAGE_SIZE, V_DIM, PAGE_SIZE), lambda bi, h, i: (bi, 0, h, 0))
    vspec = pl.BlockSpec((None, t, V_DIM), lambda bi, h, i: (bi, 0, h))
    return pl.pallas_call(
        functools.partial(_attn_prompt_body, tq=tq, lam_init=lam_init),
        grid=(b, N_HEADS, t // tq),
        in_specs=[qspec, kspec, vspec, qspec,
                  pl.BlockSpec((None, 1, V_DIM), lambda bi, h, i: (h, 0, 0)),
                  pl.BlockSpec(lamp.shape, lambda bi, h, i: (0, 0))],
        out_specs=qspec,
        out_shape=jax.ShapeDtypeStruct((b, t, d), BF16),
        scratch_shapes=[pltpu.VMEM((V_DIM, t), BF16),
                        pltpu.VMEM((2 * tq, V_DIM), BF16), pltpu.VMEM((2 * tq, V_DIM), F32),
                        pltpu.VMEM((2 * tq, 1), F32), pltpu.VMEM((2 * tq, 1), F32)],
        compiler_params=_cparams(3),
        name="attn_prompt",
    )(q, kt, v, sga, gain.reshape(N_HEADS, 1, V_DIM), lamp)


def _attn_sample_body(pt_ref, q_ref, kn_ref, vn_ref, sga_ref, gain_ref, lamp_ref, ck_ref, cv_ref, o_ref,
                      kbuf, vbuf, sem, *, layer, n_pages, s_len, lam_init):
    i = pl.program_id(0)
    n_seq = pl.num_programs(0)
    rows = N_HEADS * 2 * s_len
    d = q_ref.shape[1]

    def page_copies(seq, slot):
        out = []
        for p in range(n_pages):
            page = pt_ref[seq * n_pages + p]
            out.append(pltpu.make_async_copy(ck_ref.at[layer, page], kbuf.at[slot, p], sem.at[0, slot]))
            out.append(pltpu.make_async_copy(cv_ref.at[layer, page], vbuf.at[slot, p], sem.at[1, slot]))
        return out

    @pl.when(i == 0)
    def _():
        for cp in page_copies(0, 0):
            cp.start()

    @pl.when(i + 1 < n_seq)
    def _():
        for cp in page_copies(i + 1, (i + 1) % 2):
            cp.start()

    slot = i % 2
    for cp in page_copies(i, slot):
        cp.wait()

    qt = jnp.tile(q_ref[...], (N_HEADS * 2, 1))
    r = lax.broadcasted_iota(jnp.int32, (rows, d), 0)
    c = lax.broadcasted_iota(jnp.int32, (rows, d), 1)
    qbd = jnp.where((c // HEAD_DIM) == (r // s_len), qt, jnp.zeros_like(qt))

    s_past = [jnp.dot(qbd, kbuf[slot, p].astype(BF16), preferred_element_type=F32)
              for p in range(n_pages)]
    s_new = lax.dot_general(qbd, kn_ref[...].astype(BF16), NT_DIMS, preferred_element_type=F32)
    rn = lax.broadcasted_iota(jnp.int32, s_new.shape, 0) % s_len
    cn = lax.broadcasted_iota(jnp.int32, s_new.shape, 1)
    s_new = jnp.where(cn <= rn, s_new, NEG)

    m = jnp.max(s_new, axis=-1, keepdims=True)
    for sp in s_past:
        m = jnp.maximum(m, jnp.max(sp, axis=-1, keepdims=True))
    p_new = jnp.exp(s_new - m)
    l = jnp.sum(p_new, axis=-1, keepdims=True)
    o = jnp.dot(p_new.astype(BF16), vn_ref[...].astype(BF16), preferred_element_type=F32)
    for p in range(n_pages):
        pp = jnp.exp(s_past[p] - m)
        l = l + jnp.sum(pp, axis=-1, keepdims=True)
        v_page = jnp.concatenate([vbuf[slot, p, :, h, :] for h in range(N_HEADS)], axis=1)
        o = o + jnp.dot(pp.astype(BF16), v_page.astype(BF16), preferred_element_type=F32)
    o = o / l

    lam = _lam_value(lamp_ref, lam_init)
    for h in range(N_HEADS):
        cols = slice(h * V_DIM, (h + 1) * V_DIM)
        r0 = h * 2 * s_len
        od = o[r0:r0 + s_len, cols] - lam * o[r0 + s_len:r0 + 2 * s_len, cols]
        y = _rms(od, gain_ref[h:h + 1, :]) * (1.0 - lam_init)
        o_ref[:, cols] = (sga_ref[:, cols].astype(F32) * y).astype(BF16)


def _attn_sample(q, k_new, v_new, sga, cache_kt, cache_v, page_table, gain, lamp, *, layer, lam_init):
    n, s_len, d = q.shape
    n_pages = page_table.shape[1]
    row = pl.BlockSpec((None, s_len, d), lambda i, pt: (i, 0, 0))
    grid_spec = pltpu.PrefetchScalarGridSpec(
        num_scalar_prefetch=1,
        grid=(n,),
        in_specs=[row, row, row, row,
                  pl.BlockSpec(gain.shape, lambda i, pt: (0, 0)),
                  pl.BlockSpec(lamp.shape, lambda i, pt: (0, 0)),
                  pl.BlockSpec(memory_space=pl.ANY), pl.BlockSpec(memory_space=pl.ANY)],
        out_specs=row,
        scratch_shapes=[pltpu.VMEM((2, n_pages, d, PAGE_SIZE), F32),
                        pltpu.VMEM((2, n_pages, PAGE_SIZE, N_HEADS, V_DIM), F32),
                        pltpu.SemaphoreType.DMA((2, 2))],
    )
    return pl.pallas_call(
        functools.partial(_attn_sample_body, layer=layer, n_pages=n_pages, s_len=s_len, lam_init=lam_init),
        grid_spec=grid_spec,
        out_shape=jax.ShapeDtypeStruct((n, s_len, d), BF16),
        compiler_params=_cparams(1),
        name="attn_sample",
    )(page_table.reshape(-1), q, k_new, v_new, sga, gain, lamp, cache_kt, cache_v)


def _mix_body(ag_ref, u_ref, vn_ref, sgs_ref, x_ref, ws_ref, bt_ref, wout_ref, gffn_ref, *rest,
              with_router):
    if with_router:
        router_ref, xo_ref, h_ref, eidx_ref, gate_ref, mixed = rest
    else:
        xo_ref, h_ref, mixed = rest
    tm = x_ref.shape[0]
    rr = lax.broadcasted_iota(jnp.int32, (CHUNK, CHUNK), 0)
    cc = lax.broadcasted_iota(jnp.int32, (CHUNK, CHUNK), 1)
    tril = cc <= rr
    for g in range(N_GROUPS):
        wg = jnp.where(tril, ws_ref[g], 0.0).astype(BF16)
        bias = bt_ref[:, g:g + 1]
        cols = slice(g * GROUP_DIM, (g + 1) * GROUP_DIM)
        for ci in range(tm // CHUNK):
            rows = slice(ci * CHUNK, (ci + 1) * CHUNK)
            s = jnp.dot(wg, vn_ref[rows, cols], preferred_element_type=F32) + bias
            sgu = u_ref[rows, cols].astype(F32) * s
            mix = ag_ref[rows, cols].astype(F32) + sgs_ref[rows, cols].astype(F32) * sgu
            mixed[rows, cols] = mix.astype(BF16)
    x_new = x_ref[...] + jnp.dot(mixed[...], wout_ref[...], preferred_element_type=F32)
    xo_ref[...] = x_new
    h = _rms(x_new, gffn_ref[...])
    h_ref[...] = h.astype(h_ref.dtype)
    if with_router:
        logits = jnp.dot(h, router_ref[...], preferred_element_type=F32,
                         precision=lax.Precision.HIGHEST)
        lane = lax.broadcasted_iota(jnp.int32, logits.shape, 1)
        lg = jnp.where(lane < N_EXPERTS, logits, -jnp.inf)
        m1 = jnp.max(lg, axis=-1, keepdims=True)
        i1 = jnp.min(jnp.where(lg == m1, lane, LANES), axis=-1, keepdims=True)
        lg2 = jnp.where(lane == i1, -jnp.inf, lg)
        m2 = jnp.max(lg2, axis=-1, keepdims=True)
        i2 = jnp.min(jnp.where(lg2 == m2, lane, LANES), axis=-1, keepdims=True)
        e = jnp.exp(m2 - m1)
        g1 = 1.0 / (1.0 + e)
        g2 = e / (1.0 + e)
        eidx_ref[...] = jnp.where(lane == 0, i1, jnp.where(lane == 1, i2, 0))
        gate_ref[...] = jnp.where(lane == 0, g1, jnp.where(lane == 1, g2, 0.0))


def _mix_out(ag, u, vn, sgs, x, ws, bt, wout_bf, g_ffn, router_pad, *, tm, h_dtype):
    n, d = x.shape
    with_router = router_pad is not None
    row = pl.BlockSpec((tm, d), lambda i: (i, 0))
    lane_row = pl.BlockSpec((tm, LANES), lambda i: (i, 0))

    def full(a):
        return pl.BlockSpec(a.shape, lambda i: (0,) * a.ndim)

    g2 = g_ffn.reshape(1, d)
    in_specs = [row, row, row, row, row, full(ws), full(bt), full(wout_bf), full(g2)]
    args = [ag, u, vn, sgs, x, ws, bt, wout_bf, g2]
    out_specs = [row, row]
    out_shape = [jax.ShapeDtypeStruct((n, d), F32), jax.ShapeDtypeStruct((n, d), h_dtype)]
    if with_router:
        in_specs.append(full(router_pad))
        args.append(router_pad)
        out_specs += [lane_row, lane_row]
        out_shape += [jax.ShapeDtypeStruct((n, LANES), jnp.int32), jax.ShapeDtypeStruct((n, LANES), F32)]
    return pl.pallas_call(
        functools.partial(_mix_body, with_router=with_router),
        grid=(n // tm,),
        in_specs=in_specs,
        out_specs=out_specs,
        out_shape=out_shape,
        scratch_shapes=[pltpu.VMEM((tm, d), BF16)],
        compiler_params=_cparams(1),
        name="mix_out",
    )(*args)


def _ffn_body(te_ref, nv_ref, x_ref, wg_ref, wu_ref, wd_ref, *rest, with_res):
    del te_ref
    if with_res:
        res_ref, o_ref, hb = rest
    else:
        o_ref, hb = rest
    i = pl.program_id(0)
    j = pl.program_id(1)
    valid = i < nv_ref[0]

    @pl.when(valid)
    def _():
        @pl.when(j == 0)
        def _():
            hb[...] = x_ref[...].astype(BF16)

        h = hb[...]
        a = jax.nn.silu(jnp.dot(h, wg_ref[...], preferred_element_type=F32))
        a = (a * jnp.dot(h, wu_ref[...], preferred_element_type=F32)).astype(BF16)
        y = jnp.dot(a, wd_ref[...], preferred_element_type=F32)

        @pl.when(j == 0)
        def _():
            o_ref[...] = (res_ref[...] + y) if with_res else y

        @pl.when(j > 0)
        def _():
            o_ref[...] += y

    @pl.when(jnp.logical_and(jnp.logical_not(valid), j == 0))
    def _():
        o_ref[...] = jnp.zeros(o_ref.shape, F32)


def _ffn(x, wg, wu, wd, tile_expert, n_valid, res, *, tm, tf):
    n, d = x.shape
    f = wg.shape[2]
    with_res = res is not None
    row = pl.BlockSpec((tm, d), lambda i, j, te, nv: (i, 0))
    in_specs = [row,
                pl.BlockSpec((None, d, tf), lambda i, j, te, nv: (te[i], 0, j)),
                pl.BlockSpec((None, d, tf), lambda i, j, te, nv: (te[i], 0, j)),
                pl.BlockSpec((None, tf, d), lambda i, j, te, nv: (te[i], j, 0))]
    args = [x, wg, wu, wd]
    if with_res:
        in_specs.append(row)
        args.append(res)
    grid_spec = pltpu.PrefetchScalarGridSpec(
        num_scalar_prefetch=2,
        grid=(n // tm, f // tf),
        in_specs=in_specs,
        out_specs=row,
        scratch_shapes=[pltpu.VMEM((tm, d), BF16)],
    )
    return pl.pallas_call(
        functools.partial(_ffn_body, with_res=with_res),
        grid_spec=grid_spec,
        out_shape=jax.ShapeDtypeStruct((n, d), F32),
        compiler_params=_cparams(2),
        name="ffn",
    )(tile_expert, n_valid, *args)


GATHER_WINDOW = 64


def _gather_body(idx_ref, src_ref, dst_ref, sem, *, n_rows):
    def row_copy(src_row, dst_row):
        return pltpu.make_async_copy(src_ref.at[pl.ds(src_row, 1)], dst_ref.at[pl.ds(dst_row, 1)], sem)

    def issue(s, carry):
        row_copy(idx_ref[s], s).start()

        @pl.when(s >= GATHER_WINDOW)
        def _():
            row_copy(0, 0).wait()

        return carry

    def drain(s, carry):
        row_copy(0, 0).wait()
        return carry

    lax.fori_loop(0, n_rows, issue, 0)
    lax.fori_loop(0, min(GATHER_WINDOW, n_rows), drain, 0)


def _gather_rows(src, idx):
    n_rows = idx.shape[0]
    return pl.pallas_call(
        functools.partial(_gather_body, n_rows=n_rows),
        in_specs=[pl.BlockSpec(memory_space=pltpu.SMEM), pl.BlockSpec(memory_space=pl.ANY)],
        out_specs=pl.BlockSpec(memory_space=pl.ANY),
        out_shape=jax.ShapeDtypeStruct((n_rows, src.shape[1]), src.dtype),
        scratch_shapes=[pltpu.SemaphoreType.DMA(())],
        name="gather_rows",
    )(idx, src)


def _combine_body(x_ref, y0_ref, y1_ref, gate_ref, gfin_ref, o_ref, *, final_norm):
    g = gate_ref[...]
    x = x_ref[...] + g[:, 0:1] * y0_ref[...] + g[:, 1:2] * y1_ref[...]
    o_ref[...] = _rms(x, gfin_ref[...]) if final_norm else x


def _combine(x, y0, y1, gates, g_final, *, tm, final_norm):
    n, d = x.shape
    row = pl.BlockSpec((tm, d), lambda i: (i, 0))
    return pl.pallas_call(
        functools.partial(_combine_body, final_norm=final_norm),
        grid=(n // tm,),
        in_specs=[row, row, row, pl.BlockSpec((tm, LANES), lambda i: (i, 0)),
                  pl.BlockSpec((1, d), lambda i: (0, 0))],
        out_specs=row,
        out_shape=jax.ShapeDtypeStruct((n, d), F32),
        compiler_params=_cparams(1),
        name="combine",
    )(x, y0, y1, gates, g_final.reshape(1, d))


def _final_norm_body(x_ref, g_ref, o_ref):
    o_ref[...] = _rms(x_ref[...], g_ref[...])


def _final_norm(x, g_final, *, tm):
    n, d = x.shape
    row = pl.BlockSpec((tm, d), lambda i: (i, 0))
    return pl.pallas_call(
        _final_norm_body, grid=(n // tm,),
        in_specs=[row, pl.BlockSpec((1, d), lambda i: (0, 0))], out_specs=row,
        out_shape=jax.ShapeDtypeStruct((n, d), F32), compiler_params=_cparams(1),
        name="final_norm",
    )(x, g_final.reshape(1, d))


def _route(eidx, *, tm):
    n = eidx.shape[0]
    n_tiles = (n * TOP_K) // tm + N_EXPERTS
    flat_e = eidx.reshape(-1)
    onehot = (flat_e[:, None] == jnp.arange(N_EXPERTS, dtype=jnp.int32)[None, :]).astype(jnp.int32)
    csum = jnp.cumsum(onehot, axis=0)
    rank = jnp.sum(csum * onehot, axis=1) - 1
    counts = csum[-1]
    tiles_per = (counts + tm - 1) // tm
    tile_end = jnp.cumsum(tiles_per)
    offs = (tile_end - tiles_per) * tm
    slot = jnp.sum(onehot * offs[None, :], axis=1) + rank
    src_row = jnp.zeros((n_tiles * tm,), jnp.int32).at[slot].set(
        jnp.arange(n * TOP_K, dtype=jnp.int32) // TOP_K)
    n_valid = tile_end[-1:].astype(jnp.int32)
    tile_expert = jnp.searchsorted(tile_end, jnp.arange(n_tiles, dtype=jnp.int32), side="right")
    tile_expert = jnp.minimum(tile_expert, jnp.searchsorted(tile_end, n_valid[0] - 1, side="right"))
    return src_row, slot.reshape(n, TOP_K).astype(jnp.int32), tile_expert.astype(jnp.int32), n_valid


def kernel(x_prompt, x_sample, cache_k, cache_v, page_table, norm_mix, w_in, lam_q1, lam_k1, lam_q2, lam_k2,
           attn_head_gain, sgu_ln_g, sgu_ln_b, sgu_w, sgu_b, w_out, norm_ffn, ffn_w_gate, ffn_w_up,
           ffn_w_down, moe_router, moe_w_gate, moe_w_up, moe_w_down, norm_final):
    depth = w_in.shape[0]
    b, t, d = x_prompt.shape
    n_s, s_len, _ = x_sample.shape
    n_p = b * t
    n_sr = n_s * s_len
    n_pool = cache_k.shape[1]
    assert depth % 2 == 0, "the last layer is expected to be a routed layer"
    assert CHUNK % s_len == 0

    n_pg = t // PAGE_SIZE
    cache_kt = jnp.swapaxes(cache_k.reshape(depth, n_pool, PAGE_SIZE, d), 2, 3)
    xp = x_prompt.reshape(n_p, d)
    xs = x_sample.reshape(n_sr, d)
    tm_p, tm_s = 512, 512

    k_p, v_p, k_s, v_s, sv_s = [], [], [], [], []
    for l in range(depth):
        lam_init = 0.8 - 0.6 * math.exp(-0.3 * l)
        lamp = jnp.stack([lam_q1[l], lam_k1[l], lam_q2[l], lam_k2[l]])
        w_in_bf = w_in[l].astype(BF16)
        w_out_bf = w_out[l].astype(BF16)
        gain = attn_head_gain[l]
        ws_p = sgu_w[l]
        bt_p = sgu_b[l].T
        reps = CHUNK // s_len
        eye = jnp.eye(reps, dtype=F32)
        ws_s = jax.vmap(lambda w: jnp.kron(eye, w))(sgu_w[l][:, :s_len, :s_len])
        bt_s = jnp.tile(sgu_b[l][:, :s_len].T, (reps, 1))
        routed = l % 2 == 1
        j = l // 2
        router_pad = None
        if routed:
            router_pad = jnp.zeros((d, LANES), F32).at[:, :N_EXPERTS].set(moe_router[j])

        q, k, v, u, vn, sga, sgs, v_bf = _proj(xp, norm_mix[l], w_in_bf, sgu_ln_g[l], sgu_ln_b[l],
                                               tm=tm_p, paged=True)
        ag = _attn_prompt(q.reshape(b, t, d), k.reshape(b, n_pg, d, PAGE_SIZE), v_bf.reshape(b, t, d),
                          sga.reshape(b, t, d), gain, lamp, lam_init=lam_init, tq=512).reshape(n_p, d)
        mix_p = _mix_out(ag, u, vn, sgs, xp, ws_p, bt_p, w_out_bf, norm_ffn[l], router_pad,
                         tm=tm_p, h_dtype=F32 if routed else BF16)
        k_p.append(k)
        v_p.append(v)

        q, k, v, u, vn, sga, sgs, vnf = _proj(xs, norm_mix[l], w_in_bf, sgu_ln_g[l], sgu_ln_b[l],
                                              tm=tm_s, paged=False)
        ag = _attn_sample(q.reshape(n_s, s_len, d), k.reshape(n_s, s_len, d), v.reshape(n_s, s_len, d),
                          sga.reshape(n_s, s_len, d), cache_kt, cache_v, page_table, gain, lamp,
                          layer=l, lam_init=lam_init).reshape(n_sr, d)
        mix_s = _mix_out(ag, u, vn, sgs, xs, ws_s, bt_s, w_out_bf, norm_ffn[l], router_pad,
                         tm=tm_s, h_dtype=F32 if routed else BF16)
        k_s.append(k)
        v_s.append(v)
        sv_s.append(vnf)

        if not routed:
            wg, wu, wd = (ffn_w_gate[j:j + 1].astype(BF16), ffn_w_up[j:j + 1].astype(BF16),
                          ffn_w_down[j:j + 1].astype(BF16))
            tf = wg.shape[2] // 2
            outs = []
            for (x_new, h), tm in ((mix_p, tm_p), (mix_s, tm_s)):
                nt = x_new.shape[0] // tm
                outs.append(_ffn(h, wg, wu, wd, jnp.zeros((nt,), jnp.int32), jnp.full((1,), nt, jnp.int32),
                                 x_new, tm=tm, tf=tf))
            xp, xs = outs
        else:
            tm = 512
            x_all = jnp.concatenate([mix_p[0], mix_s[0]])
            h_all = jnp.concatenate([mix_p[1], mix_s[1]])
            eidx = jnp.concatenate([mix_p[2], mix_s[2]])[:, :TOP_K]
            gates = jnp.concatenate([mix_p[3], mix_s[3]])
            src_row, slot, tile_expert, n_valid = _route(eidx, tm=tm)
            hs = _gather_rows(h_all, src_row)
            ys = _ffn(hs, moe_w_gate[j].astype(BF16), moe_w_up[j].astype(BF16), moe_w_down[j].astype(BF16),
                      tile_expert, n_valid, None, tm=tm, tf=512)
            y0 = _gather_rows(ys, slot[:, 0])
            y1 = _gather_rows(ys, slot[:, 1])
            last = l == depth - 1
            x_all = _combine(x_all, y0, y1, gates, norm_final, tm=tm, final_norm=last)
            xp, xs = x_all[:n_p], x_all[n_p:]

    y_prompt = xp.reshape(b, t, d)
    y_sample = xs.reshape(n_s, s_len, d)
    k_prompt = jnp.moveaxis(jnp.stack(k_p).reshape(depth, b, n_pg, N_HEADS, 2, HEAD_DIM, PAGE_SIZE), 6, 3)
    v_prompt = jnp.stack(v_p).reshape(depth, b, n_pg, PAGE_SIZE, N_HEADS, V_DIM)
    k_sample = jnp.stack(k_s).reshape(depth, n_s, s_len, N_HEADS, 2, HEAD_DIM)
    v_sample = jnp.stack(v_s).reshape(depth, n_s, s_len, N_HEADS, V_DIM)
    sgu_v_sample = jnp.stack(sv_s).reshape(depth, n_s, s_len, d)
    return (y_prompt, y_sample, k_prompt, v_prompt, k_sample, v_sample, sgu_v_sample)
```

```python
import functools
import math

import jax
import jax.numpy as jnp
from jax import lax
from jax.experimental import pallas as pl
from jax.experimental.pallas import tpu as pltpu

F32 = jnp.float32
BF16 = jnp.bfloat16
I32 = jnp.int32

N_HEADS = 8
HEAD_DIM = 64
V_DIM = 2 * HEAD_DIM
CHUNK = 128
N_GROUPS = 8
GROUP_DIM = 128
N_EXPERTS = 8
TOP_K = 2
PAGE_SIZE = 128
EPS = 1e-6
NEG = -1e30
QK_SCALE = HEAD_DIM ** -0.5
N_SEG = 7
LANES = 128
VMEM_LIMIT = 56 * 1024 * 1024

NT_DIMS = (((1,), (1,)), ((), ()))


def _cparams(n_axes, vmem=VMEM_LIMIT):
    return pltpu.CompilerParams(dimension_semantics=("arbitrary",) * n_axes,
                                vmem_limit_bytes=vmem)


def _rms(x, gain):
    return x * lax.rsqrt(jnp.mean(x * x, axis=-1, keepdims=True) + EPS) * gain


def _lam_value(lamp_ref, lam_init):
    a = jnp.sum(lamp_ref[0:1, :] * lamp_ref[1:2, :], axis=-1, keepdims=True)
    b = jnp.sum(lamp_ref[2:3, :] * lamp_ref[3:4, :], axis=-1, keepdims=True)
    return jnp.exp(a) - jnp.exp(b) + lam_init


def _one_hot_cols(rel, width):
    lane = lax.broadcasted_iota(I32, rel.shape, 1)
    return jnp.concatenate([jnp.where(rel == lane + q * LANES, 1.0, 0.0) for q in range(width // LANES)], axis=1)


def _proj_body(x_ref, g_ref, w_ref, lng_ref, lnb_ref,
               q_ref, k_ref, v_ref, u_ref, vn_ref, sga_ref, sgs_ref, extra_ref, *, paged):
    tm, d = x_ref.shape
    h = _rms(x_ref[...], g_ref[...]).astype(BF16)

    def seg(s):
        return jnp.dot(h, w_ref[:, s * d:(s + 1) * d], preferred_element_type=F32)

    q_ref[...] = (seg(0) * QK_SCALE).astype(BF16)
    if paged:
        kk = seg(1)
        for pg in range(tm // PAGE_SIZE):
            k_ref[pg] = kk[pg * PAGE_SIZE:(pg + 1) * PAGE_SIZE, :].T
        vv = seg(2)
        for hd in range(N_HEADS):
            v_ref[:, hd, :] = vv[:, hd * V_DIM:(hd + 1) * V_DIM]
        extra_ref[...] = vv.astype(BF16)
    else:
        k_ref[...] = seg(1)
        v_ref[...] = seg(2)
    u_ref[...] = jax.nn.gelu(seg(3)).astype(BF16)
    vg = jax.nn.gelu(seg(4))
    xc = vg - jnp.mean(vg, axis=-1, keepdims=True)
    vn = xc * lax.rsqrt(jnp.mean(xc * xc, axis=-1, keepdims=True) + EPS) * lng_ref[...] + lnb_ref[...]
    vn_ref[...] = vn.astype(BF16)
    if not paged:
        extra_ref[...] = vn
    sga_ref[...] = jax.nn.sigmoid(seg(5)).astype(BF16)
    sgs_ref[...] = jax.nn.sigmoid(seg(6)).astype(BF16)


def _proj(x, gain, w_bf, ln_g, ln_b, *, tm, paged):
    n, d = x.shape
    row = pl.BlockSpec((tm, d), lambda i: (i, 0))
    vec = pl.BlockSpec((1, d), lambda i: (0, 0))
    rows_bf = jax.ShapeDtypeStruct((n, d), BF16)
    rows_f32 = jax.ShapeDtypeStruct((n, d), F32)
    if paged:
        k_spec = pl.BlockSpec((tm // PAGE_SIZE, d, PAGE_SIZE), lambda i: (i, 0, 0))
        k_shape = jax.ShapeDtypeStruct((n // PAGE_SIZE, d, PAGE_SIZE), F32)
        v_spec = pl.BlockSpec((tm, N_HEADS, V_DIM), lambda i: (i, 0, 0))
        v_shape = jax.ShapeDtypeStruct((n, N_HEADS, V_DIM), F32)
    else:
        k_spec, k_shape, v_spec, v_shape = row, rows_f32, row, rows_f32
    out_specs = [row, k_spec, v_spec, row, row, row, row, row]
    out_shape = [rows_bf, k_shape, v_shape, rows_bf, rows_bf, rows_bf, rows_bf, rows_bf if paged else rows_f32]
    return pl.pallas_call(
        functools.partial(_proj_body, paged=paged),
        grid=(n // tm,),
        in_specs=[row, vec, pl.BlockSpec((d, N_SEG * d), lambda i: (0, 0)), vec, vec],
        out_specs=out_specs,
        out_shape=out_shape,
        compiler_params=_cparams(1),
        name="proj",
    )(x, gain.reshape(1, d), w_bf, ln_g.reshape(1, d), ln_b.reshape(1, d))


def _attn_prompt_body(q_ref, k_ref, vb, sga_ref, gain_ref, lamp_ref, o_ref,
                      kb, qs, acc, m_s, l_s, *, tq, lam_init):
    i = pl.program_id(2)

    @pl.when(i == 0)
    def _():
        for pg in range(k_ref.shape[0]):
            kb[:, pg * PAGE_SIZE:(pg + 1) * PAGE_SIZE] = k_ref[pg].astype(BF16)

    q = q_ref[...]
    lane = lax.broadcasted_iota(I32, q.shape, 1)
    zero = jnp.zeros_like(q)
    qs[0:tq, :] = jnp.where(lane < HEAD_DIM, q, zero)
    qs[tq:, :] = jnp.where(lane >= HEAD_DIM, q, zero)
    m_s[...] = jnp.full(m_s.shape, NEG, F32)
    l_s[...] = jnp.zeros(l_s.shape, F32)
    acc[...] = jnp.zeros(acc.shape, F32)

    def step(j, masked):
        start = pl.multiple_of(j * tq, tq)
        s = jnp.dot(qs[...], kb[:, pl.ds(start, tq)], preferred_element_type=F32)
        if masked:
            r = lax.broadcasted_iota(I32, s.shape, 0)
            r = jnp.where(r >= tq, r - tq, r)
            c = lax.broadcasted_iota(I32, s.shape, 1)
            s = jnp.where(c <= r, s, NEG)
        m_prev = m_s[...]
        m_new = jnp.maximum(m_prev, jnp.max(s, axis=-1, keepdims=True))
        alpha = jnp.exp(m_prev - m_new)
        p = jnp.exp(s - jnp.tile(m_new, (1, tq // LANES)))
        l_s[...] = alpha * l_s[...] + jnp.sum(p, axis=-1, keepdims=True)
        acc[...] = alpha * acc[...] + jnp.dot(p.astype(BF16), vb[pl.ds(start, tq), :],
                                              preferred_element_type=F32)
        m_s[...] = m_new

    def off_diag(j, carry):
        step(j, False)
        return carry

    lax.fori_loop(0, i, off_diag, 0)
    step(i, True)

    o = acc[...] / l_s[...]
    lam = _lam_value(lamp_ref, lam_init)
    od = o[0:tq, :] - lam * o[tq:, :]
    y = _rms(od, gain_ref[...]) * (1.0 - lam_init)
    o_ref[...] = (sga_ref[...].astype(F32) * y).astype(BF16)


def _attn_prompt(q, kt, v, sga, gain, lamp, *, lam_init, tq):
    b, t, d = q.shape
    assert V_DIM == LANES
    qspec = pl.BlockSpec((None, tq, V_DIM), lambda bi, h, i: (bi, i, h))
    kspec = pl.BlockSpec((None, t // PAGE_SIZE, V_DIM, PAGE_SIZE), lambda bi, h, i: (bi, 0, h, 0))
    vspec = pl.BlockSpec((None, t, V_DIM), lambda bi, h, i: (bi, 0, h))
    return pl.pallas_call(
        functools.partial(_attn_prompt_body, tq=tq, lam_init=lam_init),
        grid=(b, N_HEADS, t // tq),
        in_specs=[qspec, kspec, vspec, qspec,
                  pl.BlockSpec((None, 1, V_DIM), lambda bi, h, i: (h, 0, 0)),
                  pl.BlockSpec(lamp.shape, lambda bi, h, i: (0, 0))],
        out_specs=qspec,
        out_shape=jax.ShapeDtypeStruct((b, t, d), BF16),
        scratch_shapes=[pltpu.VMEM((V_DIM, t), BF16),
                        pltpu.VMEM((2 * tq, V_DIM), BF16), pltpu.VMEM((2 * tq, V_DIM), F32),
                        pltpu.VMEM((2 * tq, LANES), F32), pltpu.VMEM((2 * tq, LANES), F32)],
        compiler_params=_cparams(3),
        name="attn_prompt",
    )(q, kt, v, sga, gain.reshape(N_HEADS, 1, V_DIM), lamp)


def _attn_sample_body(pt_ref, q_ref, kn_ref, vn_ref, sga_ref, gain_ref, lamp_ref, ck_ref, cv_ref, o_ref,
                      kbuf, vbuf, sem, *, layer, n_pages, s_len, lam_init):
    i = pl.program_id(0)
    n_seq = pl.num_programs(0)
    rows = N_HEADS * 2 * s_len
    d = q_ref.shape[1]

    def page_copies(seq, slot):
        out = []
        for p in range(n_pages):
            page = pt_ref[seq * n_pages + p]
            out.append(pltpu.make_async_copy(ck_ref.at[layer, page], kbuf.at[slot, p], sem.at[0, slot]))
            out.append(pltpu.make_async_copy(cv_ref.at[layer, page], vbuf.at[slot, p], sem.at[1, slot]))
        return out

    @pl.when(i == 0)
    def _():
        for cp in page_copies(0, 0):
            cp.start()

    @pl.when(i + 1 < n_seq)
    def _():
        for cp in page_copies(i + 1, (i + 1) % 2):
            cp.start()

    slot = i % 2
    for cp in page_copies(i, slot):
        cp.wait()

    qt = jnp.tile(q_ref[...], (N_HEADS * 2, 1))
    r = lax.broadcasted_iota(I32, (rows, d), 0)
    c = lax.broadcasted_iota(I32, (rows, d), 1)
    qbd = jnp.where((c // HEAD_DIM) == (r // s_len), qt, jnp.zeros_like(qt))

    s_past = [jnp.dot(qbd, kbuf[slot, p].astype(BF16), preferred_element_type=F32)
              for p in range(n_pages)]
    s_new = lax.dot_general(qbd, kn_ref[...].astype(BF16), NT_DIMS, preferred_element_type=F32)
    rn = lax.broadcasted_iota(I32, s_new.shape, 0) % s_len
    cn = lax.broadcasted_iota(I32, s_new.shape, 1)
    s_new = jnp.where(cn <= rn, s_new, NEG)

    m = jnp.max(s_new, axis=-1, keepdims=True)
    for sp in s_past:
        m = jnp.maximum(m, jnp.max(sp, axis=-1, keepdims=True))
    p_new = jnp.exp(s_new - m)
    l = jnp.sum(p_new, axis=-1, keepdims=True)
    o = jnp.dot(p_new.astype(BF16), vn_ref[...].astype(BF16), preferred_element_type=F32)
    for p in range(n_pages):
        pp = jnp.exp(s_past[p] - m)
        l = l + jnp.sum(pp, axis=-1, keepdims=True)
        v_page = jnp.concatenate([vbuf[slot, p, :, h, :] for h in range(N_HEADS)], axis=1)
        o = o + jnp.dot(pp.astype(BF16), v_page.astype(BF16), preferred_element_type=F32)
    o = o / l

    lam = _lam_value(lamp_ref, lam_init)
    for h in range(N_HEADS):
        cols = slice(h * V_DIM, (h + 1) * V_DIM)
        r0 = h * 2 * s_len
        od = o[r0:r0 + s_len, cols] - lam * o[r0 + s_len:r0 + 2 * s_len, cols]
        y = _rms(od, gain_ref[h:h + 1, :]) * (1.0 - lam_init)
        o_ref[:, cols] = (sga_ref[:, cols].astype(F32) * y).astype(BF16)


def _attn_sample(q, k_new, v_new, sga, cache_kt, cache_v, page_table, gain, lamp, *, layer, lam_init):
    n, s_len, d = q.shape
    n_pages = page_table.shape[1]
    row = pl.BlockSpec((None, s_len, d), lambda i, pt: (i, 0, 0))
    grid_spec = pltpu.PrefetchScalarGridSpec(
        num_scalar_prefetch=1,
        grid=(n,),
        in_specs=[row, row, row, row,
                  pl.BlockSpec(gain.shape, lambda i, pt: (0, 0)),
                  pl.BlockSpec(lamp.shape, lambda i, pt: (0, 0)),
                  pl.BlockSpec(memory_space=pl.ANY), pl.BlockSpec(memory_space=pl.ANY)],
        out_specs=row,
        scratch_shapes=[pltpu.VMEM((2, n_pages, d, PAGE_SIZE), F32),
                        pltpu.VMEM((2, n_pages, PAGE_SIZE, N_HEADS, V_DIM), F32),
                        pltpu.SemaphoreType.DMA((2, 2))],
    )
    return pl.pallas_call(
        functools.partial(_attn_sample_body, layer=layer, n_pages=n_pages, s_len=s_len, lam_init=lam_init),
        grid_spec=grid_spec,
        out_shape=jax.ShapeDtypeStruct((n, s_len, d), BF16),
        compiler_params=_cparams(1),
        name="attn_sample",
    )(page_table.reshape(-1), q, k_new, v_new, sga, gain, lamp, cache_kt, cache_v)


def _mix_body(ag_ref, u_ref, vn_ref, sgs_ref, x_ref, ws_ref, bt_ref, wout_ref, gffn_ref, *rest,
              with_router):
    if with_router:
        router_ref, xo_ref, h_ref, eidx_ref, gate_ref, mixed = rest
    else:
        xo_ref, h_ref, mixed = rest
    tm = x_ref.shape[0]
    rr = lax.broadcasted_iota(I32, (CHUNK, CHUNK), 0)
    cc = lax.broadcasted_iota(I32, (CHUNK, CHUNK), 1)
    tril = cc <= rr
    for g in range(N_GROUPS):
        wg = jnp.where(tril, ws_ref[g], 0.0).astype(BF16)
        bias = bt_ref[:, g:g + 1]
        cols = slice(g * GROUP_DIM, (g + 1) * GROUP_DIM)
        for ci in range(tm // CHUNK):
            rows = slice(ci * CHUNK, (ci + 1) * CHUNK)
            s = jnp.dot(wg, vn_ref[rows, cols], preferred_element_type=F32) + bias
            sgu = u_ref[rows, cols].astype(F32) * s
            mix = ag_ref[rows, cols].astype(F32) + sgs_ref[rows, cols].astype(F32) * sgu
            mixed[rows, cols] = mix.astype(BF16)
    x_new = x_ref[...] + jnp.dot(mixed[...], wout_ref[...], preferred_element_type=F32)
    xo_ref[...] = x_new
    h = _rms(x_new, gffn_ref[...])
    h_ref[...] = h.astype(BF16)
    if with_router:
        logits = jnp.dot(h, router_ref[...], preferred_element_type=F32,
                         precision=lax.Precision.HIGHEST)
        lane = lax.broadcasted_iota(I32, logits.shape, 1)
        lg = jnp.where(lane < N_EXPERTS, logits, -jnp.inf)
        m1 = jnp.max(lg, axis=-1, keepdims=True)
        i1 = jnp.min(jnp.where(lg == m1, lane, LANES), axis=-1, keepdims=True)
        lg2 = jnp.where(lane == i1, -jnp.inf, lg)
        m2 = jnp.max(lg2, axis=-1, keepdims=True)
        i2 = jnp.min(jnp.where(lg2 == m2, lane, LANES), axis=-1, keepdims=True)
        e = jnp.exp(m2 - m1)
        g1 = 1.0 / (1.0 + e)
        g2 = e / (1.0 + e)
        eidx_ref[...] = jnp.where(lane == 0, i1, jnp.where(lane == 1, i2, 0))
        gate_ref[...] = jnp.where(lane == 0, g1, jnp.where(lane == 1, g2, 0.0))


def _mix_out(ag, u, vn, sgs, x, ws, bt, wout_bf, g_ffn, router_pad, *, tm):
    n, d = x.shape
    with_router = router_pad is not None
    row = pl.BlockSpec((tm, d), lambda i: (i, 0))
    lane_row = pl.BlockSpec((tm, LANES), lambda i: (i, 0))

    def full(a):
        return pl.BlockSpec(a.shape, lambda i: (0,) * a.ndim)

    g2 = g_ffn.reshape(1, d)
    in_specs = [row, row, row, row, row, full(ws), full(bt), full(wout_bf), full(g2)]
    args = [ag, u, vn, sgs, x, ws, bt, wout_bf, g2]
    out_specs = [row, row]
    out_shape = [jax.ShapeDtypeStruct((n, d), F32), jax.ShapeDtypeStruct((n, d), BF16)]
    if with_router:
        in_specs.append(full(router_pad))
        args.append(router_pad)
        out_specs += [lane_row, lane_row]
        out_shape += [jax.ShapeDtypeStruct((n, LANES), I32), jax.ShapeDtypeStruct((n, LANES), F32)]
    return pl.pallas_call(
        functools.partial(_mix_body, with_router=with_router),
        grid=(n // tm,),
        in_specs=in_specs,
        out_specs=out_specs,
        out_shape=out_shape,
        scratch_shapes=[pltpu.VMEM((tm, d), BF16)],
        compiler_params=_cparams(1),
        name="mix_out",
    )(*args)


def _ffn_body(te_ref, nv_ref, x_ref, wg_ref, wu_ref, wd_ref, side_ref, o_ref, acc, *, routed):
    del te_ref
    i = pl.program_id(0)
    j = pl.program_id(1)
    last = pl.num_programs(1) - 1
    valid = i < nv_ref[0]

    @pl.when(valid)
    def _():
        h = x_ref[...]
        a = jax.nn.silu(jnp.dot(h, wg_ref[...], preferred_element_type=F32))
        a = (a * jnp.dot(h, wu_ref[...], preferred_element_type=F32)).astype(BF16)
        y = jnp.dot(a, wd_ref[...], preferred_element_type=F32)

        @pl.when(j == 0)
        def _():
            acc[...] = y

        @pl.when(j > 0)
        def _():
            acc[...] += y

        @pl.when(j == last)
        def _():
            if routed:
                o_ref[...] = (acc[...] * jnp.tile(side_ref[...], (1, acc.shape[1] // LANES))).astype(o_ref.dtype)
            else:
                o_ref[...] = side_ref[...] + acc[...]

    @pl.when(jnp.logical_and(jnp.logical_not(valid), j == last))
    def _():
        o_ref[...] = jnp.zeros(o_ref.shape, o_ref.dtype)


def _ffn(x, wg, wu, wd, tile_expert, n_valid, side, *, tm, tf, routed):
    n, d = x.shape
    f = wg.shape[2]

    def row_idx(i, j, te, nv):
        return (jnp.minimum(i, nv[0] - 1), 0)

    side_spec = pl.BlockSpec((tm, side.shape[1]), row_idx)
    grid_spec = pltpu.PrefetchScalarGridSpec(
        num_scalar_prefetch=2,
        grid=(n // tm, f // tf),
        in_specs=[pl.BlockSpec((tm, d), row_idx),
                  pl.BlockSpec((None, d, tf), lambda i, j, te, nv: (te[i], 0, j)),
                  pl.BlockSpec((None, d, tf), lambda i, j, te, nv: (te[i], 0, j)),
                  pl.BlockSpec((None, tf, d), lambda i, j, te, nv: (te[i], j, 0)),
                  side_spec],
        out_specs=pl.BlockSpec((tm, d), lambda i, j, te, nv: (i, 0)),
        scratch_shapes=[pltpu.VMEM((tm, d), F32)],
    )
    return pl.pallas_call(
        functools.partial(_ffn_body, routed=routed),
        grid_spec=grid_spec,
        out_shape=jax.ShapeDtypeStruct((n, d), BF16 if routed else F32),
        compiler_params=_cparams(2),
        name="ffn",
    )(tile_expert, n_valid, x, wg, wu, wd, side)


def _dispatch_body(tile_ref, chunk_ref, first_ref, valid_ref, src_ref, h_ref, o_ref, *, tc):
    del tile_ref
    s = pl.program_id(0)

    @pl.when(valid_ref[s] == 1)
    def _():
        rel = src_ref[...] - chunk_ref[s] * tc
        pick = _one_hot_cols(rel, tc).astype(BF16)
        rows = jnp.dot(pick, h_ref[...], preferred_element_type=F32).astype(BF16)

        @pl.when(first_ref[s] == 1)
        def _():
            o_ref[...] = rows

        @pl.when(first_ref[s] == 0)
        def _():
            o_ref[...] += rows


def _dispatch(h, src_rep, sched, *, tm, tc):
    n, d = h.shape
    n_slots = src_rep.shape[0]
    tile, chunk, first, valid = sched
    grid_spec = pltpu.PrefetchScalarGridSpec(
        num_scalar_prefetch=4,
        grid=(tile.shape[0],),
        in_specs=[pl.BlockSpec((tm, LANES), lambda s, t, c, f, v: (t[s], 0)),
                  pl.BlockSpec((tc, d), lambda s, t, c, f, v: (c[s], 0))],
        out_specs=pl.BlockSpec((tm, d), lambda s, t, c, f, v: (t[s], 0)),
    )
    return pl.pallas_call(
        functools.partial(_dispatch_body, tc=tc),
        grid_spec=grid_spec,
        out_shape=jax.ShapeDtypeStruct((n_slots, d), BF16),
        compiler_params=_cparams(1),
        name="dispatch",
    )(tile, chunk, first, valid, src_rep, h)


def _combine_body(chunk_ref, tile_ref, first_ref, last_ref, valid_ref,
                  s0_ref, s1_ref, ys_ref, x_ref, gfin_ref, o_ref, acc, *, tm, final_norm):
    del chunk_ref
    s = pl.program_id(0)

    @pl.when(first_ref[s] == 1)
    def _():
        acc[...] = jnp.zeros(acc.shape, F32)

    @pl.when(valid_ref[s] == 1)
    def _():
        base = tile_ref[s] * tm
        hit = _one_hot_cols(s0_ref[...] - base, tm) + _one_hot_cols(s1_ref[...] - base, tm)
        acc[...] += jnp.dot(hit.astype(BF16), ys_ref[...], preferred_element_type=F32)

    @pl.when(last_ref[s] == 1)
    def _():
        x = x_ref[...] + acc[...]
        o_ref[...] = _rms(x, gfin_ref[...]) if final_norm else x


def _combine(x, ys, slot0_rep, slot1_rep, g_final, sched, *, tm, tc, final_norm):
    n, d = x.shape
    chunk, tile, first, last, valid = sched
    tok = pl.BlockSpec((tc, LANES), lambda s, c, t, f, l, v: (c[s], 0))
    grid_spec = pltpu.PrefetchScalarGridSpec(
        num_scalar_prefetch=5,
        grid=(chunk.shape[0],),
        in_specs=[tok, tok,
                  pl.BlockSpec((tm, d), lambda s, c, t, f, l, v: (t[s], 0)),
                  pl.BlockSpec((tc, d), lambda s, c, t, f, l, v: (c[s], 0)),
                  pl.BlockSpec((1, d), lambda s, c, t, f, l, v: (0, 0))],
        out_specs=pl.BlockSpec((tc, d), lambda s, c, t, f, l, v: (c[s], 0)),
        scratch_shapes=[pltpu.VMEM((tc, d), F32)],
    )
    return pl.pallas_call(
        functools.partial(_combine_body, tm=tm, final_norm=final_norm),
        grid_spec=grid_spec,
        out_shape=jax.ShapeDtypeStruct((n, d), F32),
        compiler_params=_cparams(1),
        name="combine",
    )(chunk, tile, first, last, valid, slot0_rep, slot1_rep, ys, x, g_final.reshape(1, d))


def _flatten_ranges(lengths, n_steps):
    end = jnp.cumsum(lengths)
    total = end[-1]
    s = jnp.arange(n_steps, dtype=I32)
    sc = jnp.minimum(s, total - 1)
    seg = jnp.sum((end[None, :] <= sc[:, None]).astype(I32), axis=1)
    off = sc - (end - lengths)[seg]
    return seg, off, (s < total).astype(I32)


def _route(eidx, gates, *, tm, tc):
    n = eidx.shape[0]
    n_pairs = n * TOP_K
    n_chunks = n // tc
    n_tiles = n_pairs // tm + N_EXPERTS
    n_steps = n_tiles + N_EXPERTS * n_chunks
    experts = jnp.arange(N_EXPERTS, dtype=I32)

    flat_e = eidx.reshape(-1)
    onehot = (flat_e[:, None] == experts[None, :]).astype(I32)
    csum = jnp.cumsum(onehot, axis=0)
    rank = jnp.sum(csum * onehot, axis=1) - 1
    counts = csum[-1]
    tiles_per = (counts + tm - 1) // tm
    tile_end = jnp.cumsum(tiles_per)
    tile_start = tile_end - tiles_per
    offs = tile_start * tm
    slot = jnp.sum(onehot * offs[None, :], axis=1) + rank
    n_valid = tile_end[-1:]

    upd = jnp.stack([(jnp.arange(n_pairs, dtype=I32) // TOP_K).astype(F32), gates.reshape(-1)], axis=1)
    init = jnp.concatenate([jnp.full((n_tiles * tm, 1), -1.0, F32), jnp.zeros((n_tiles * tm, 1), F32)], axis=1)
    per_slot = init.at[slot].set(upd)
    src_rep = jnp.broadcast_to(per_slot[:, 0:1].astype(I32), (n_tiles * tm, LANES))
    gate_rep = jnp.broadcast_to(per_slot[:, 1:2], (n_tiles * tm, LANES))

    t = jnp.arange(n_tiles, dtype=I32)
    used = t < n_valid[0]
    te = jnp.minimum(jnp.sum((tile_end[None, :] <= t[:, None]).astype(I32), axis=1), N_EXPERTS - 1)
    tile_expert = jnp.where(used, te, te[n_valid[0] - 1])
    r_first = (t - tile_start[te]) * tm
    r_last = jnp.minimum(r_first + tm, counts[te]) - 1
    cs_t = csum[:, te]
    p_first = jnp.sum((cs_t <= r_first[None, :]).astype(I32), axis=0)
    p_last = jnp.sum((cs_t <= r_last[None, :]).astype(I32), axis=0)
    c_lo = (p_first // TOP_K) // tc
    c_hi = (p_last // TOP_K) // tc
    d_tile, d_off, d_valid = _flatten_ranges(jnp.where(used, c_hi - c_lo + 1, 1), n_steps)
    d_chunk = jnp.where(used[d_tile], c_lo[d_tile] + d_off, 0)
    d_first = d_valid * (d_off == 0).astype(I32)
    dispatch_sched = (d_tile, d_chunk, d_first, d_valid)

    before = jnp.concatenate([jnp.zeros((1, N_EXPERTS), I32), csum[TOP_K * tc - 1::TOP_K * tc]], axis=0)
    cnt = before[1:] - before[:-1]
    lo_slot = offs[None, :] + before[:-1]
    lo_tile = (lo_slot // tm).reshape(-1)
    hi_tile = ((lo_slot + cnt - 1) // tm).reshape(-1)
    span = jnp.where(cnt.reshape(-1) > 0, hi_tile - lo_tile + 1, 0)
    c_seg, c_off, c_valid = _flatten_ranges(span, n_steps)
    c_chunk = c_seg // N_EXPERTS
    c_tile = lo_tile[c_seg] + c_off
    prev_chunk = jnp.concatenate([jnp.full((1,), -1, I32), c_chunk[:-1]])
    next_chunk = jnp.concatenate([c_chunk[1:], jnp.full((1,), -1, I32)])
    next_valid = jnp.concatenate([c_valid[1:], jnp.zeros((1,), I32)])
    c_first = c_valid * (c_chunk != prev_chunk).astype(I32)
    c_last = c_valid * jnp.logical_or(c_chunk != next_chunk, next_valid == 0).astype(I32)
    combine_sched = (c_chunk, c_tile, c_first, c_last, c_valid)

    slot2 = slot.reshape(n, TOP_K)
    slot0_rep = jnp.broadcast_to(slot2[:, 0:1], (n, LANES))
    slot1_rep = jnp.broadcast_to(slot2[:, 1:2], (n, LANES))
    return src_rep, gate_rep, tile_expert, n_valid, dispatch_sched, slot0_rep, slot1_rep, combine_sched


def kernel(x_prompt, x_sample, cache_k, cache_v, page_table, norm_mix, w_in, lam_q1, lam_k1, lam_q2, lam_k2,
           attn_head_gain, sgu_ln_g, sgu_ln_b, sgu_w, sgu_b, w_out, norm_ffn, ffn_w_gate, ffn_w_up,
           ffn_w_down, moe_router, moe_w_gate, moe_w_up, moe_w_down, norm_final):
    depth = w_in.shape[0]
    b, t, d = x_prompt.shape
    n_s, s_len, _ = x_sample.shape
    n_p = b * t
    n_sr = n_s * s_len
    n_pool = cache_k.shape[1]
    assert depth % 2 == 0, "the last layer is expected to be a routed layer"
    assert CHUNK % s_len == 0

    n_pg = t // PAGE_SIZE
    cache_kt = jnp.swapaxes(cache_k.reshape(depth, n_pool, PAGE_SIZE, d), 2, 3)
    xp = x_prompt.reshape(n_p, d)
    xs = x_sample.reshape(n_sr, d)
    tm_p, tm_s = 512, 512

    k_p, v_p, k_s, v_s, sv_s = [], [], [], [], []
    for l in range(depth):
        lam_init = 0.8 - 0.6 * math.exp(-0.3 * l)
        lamp = jnp.stack([lam_q1[l], lam_k1[l], lam_q2[l], lam_k2[l]])
        w_in_bf = w_in[l].astype(BF16)
        w_out_bf = w_out[l].astype(BF16)
        gain = attn_head_gain[l]
        ws_p = sgu_w[l]
        bt_p = sgu_b[l].T
        reps = CHUNK // s_len
        eye = jnp.eye(reps, dtype=F32)
        ws_s = jax.vmap(lambda w: jnp.kron(eye, w))(sgu_w[l][:, :s_len, :s_len])
        bt_s = jnp.tile(sgu_b[l][:, :s_len].T, (reps, 1))
        routed = l % 2 == 1
        j = l // 2
        router_pad = None
        if routed:
            router_pad = jnp.zeros((d, LANES), F32).at[:, :N_EXPERTS].set(moe_router[j])

        q, k, v, u, vn, sga, sgs, v_bf = _proj(xp, norm_mix[l], w_in_bf, sgu_ln_g[l], sgu_ln_b[l],
                                               tm=tm_p, paged=True)
        ag = _attn_prompt(q.reshape(b, t, d), k.reshape(b, n_pg, d, PAGE_SIZE), v_bf.reshape(b, t, d),
                          sga.reshape(b, t, d), gain, lamp, lam_init=lam_init, tq=512).reshape(n_p, d)
        mix_p = _mix_out(ag, u, vn, sgs, xp, ws_p, bt_p, w_out_bf, norm_ffn[l], router_pad, tm=tm_p)
        k_p.append(k)
        v_p.append(v)

        q, k, v, u, vn, sga, sgs, vnf = _proj(xs, norm_mix[l], w_in_bf, sgu_ln_g[l], sgu_ln_b[l],
                                              tm=tm_s, paged=False)
        ag = _attn_sample(q.reshape(n_s, s_len, d), k.reshape(n_s, s_len, d), v.reshape(n_s, s_len, d),
                          sga.reshape(n_s, s_len, d), cache_kt, cache_v, page_table, gain, lamp,
                          layer=l, lam_init=lam_init).reshape(n_sr, d)
        mix_s = _mix_out(ag, u, vn, sgs, xs, ws_s, bt_s, w_out_bf, norm_ffn[l], router_pad, tm=tm_s)
        k_s.append(k)
        v_s.append(v)
        sv_s.append(vnf)

        if not routed:
            wg, wu, wd = (ffn_w_gate[j:j + 1].astype(BF16), ffn_w_up[j:j + 1].astype(BF16),
                          ffn_w_down[j:j + 1].astype(BF16))
            tf = wg.shape[2] // 2
            outs = []
            for (x_new, h), tm in ((mix_p, tm_p), (mix_s, tm_s)):
                nt = x_new.shape[0] // tm
                outs.append(_ffn(h, wg, wu, wd, jnp.zeros((nt,), I32), jnp.full((1,), nt, I32),
                                 x_new, tm=tm, tf=tf, routed=False))
            xp, xs = outs
        else:
            tm = tc = 512
            x_all = jnp.concatenate([mix_p[0], mix_s[0]])
            h_all = jnp.concatenate([mix_p[1], mix_s[1]])
            eidx = jnp.concatenate([mix_p[2], mix_s[2]])[:, :TOP_K]
            gates = jnp.concatenate([mix_p[3], mix_s[3]])[:, :TOP_K]
            (src_rep, gate_rep, tile_expert, n_valid, dispatch_sched,
             slot0_rep, slot1_rep, combine_sched) = _route(eidx, gates, tm=tm, tc=tc)
            hs = _dispatch(h_all, src_rep, dispatch_sched, tm=tm, tc=tc)
            ys = _ffn(hs, moe_w_gate[j].astype(BF16), moe_w_up[j].astype(BF16), moe_w_down[j].astype(BF16),
                      tile_expert, n_valid, gate_rep, tm=tm, tf=512, routed=True)
            x_all = _combine(x_all, ys, slot0_rep, slot1_rep, norm_final, combine_sched,
                             tm=tm, tc=tc, final_norm=(l == depth - 1))
            xp, xs = x_all[:n_p], x_all[n_p:]

    y_prompt = xp.reshape(b, t, d)
    y_sample = xs.reshape(n_s, s_len, d)
    k_prompt = jnp.moveaxis(jnp.stack(k_p).reshape(depth, b, n_pg, N_HEADS, 2, HEAD_DIM, PAGE_SIZE), 6, 3)
    v_prompt = jnp.stack(v_p).reshape(depth, b, n_pg, PAGE_SIZE, N_HEADS, V_DIM)
    k_sample = jnp.stack(k_s).reshape(depth, n_s, s_len, N_HEADS, 2, HEAD_DIM)
    v_sample = jnp.stack(v_s).reshape(depth, n_s, s_len, N_HEADS, V_DIM)
    sgu_v_sample = jnp.stack(sv_s).reshape(depth, n_s, s_len, d)
    return (y_prompt, y_sample, k_prompt, v_prompt, k_sample, v_sample, sgu_v_sample)
```

```python
import functools
import math

import jax
import jax.numpy as jnp
from jax import lax
from jax.experimental import pallas as pl
from jax.experimental.pallas import tpu as pltpu

F32 = jnp.float32
BF16 = jnp.bfloat16
I32 = jnp.int32

N_HEADS = 8
HEAD_DIM = 64
V_DIM = 2 * HEAD_DIM
CHUNK = 128
N_GROUPS = 8
GROUP_DIM = 128
N_EXPERTS = 8
TOP_K = 2
PAGE_SIZE = 128
EPS = 1e-6
NEG = -1e30
Q_PRESCALE = HEAD_DIM ** -0.5 * math.log2(math.e)
N_SEG = 7
LANES = 128
VMEM_LIMIT = 56 * 1024 * 1024

NT_DIMS = (((1,), (1,)), ((), ()))


def _cparams(n_axes, vmem=VMEM_LIMIT):
    return pltpu.CompilerParams(dimension_semantics=("arbitrary",) * n_axes,
                                vmem_limit_bytes=vmem)


def _rms(x, gain):
    return x * lax.rsqrt(jnp.mean(x * x, axis=-1, keepdims=True) + EPS) * gain


def _lam_value(lamp_ref, lam_init):
    a = jnp.sum(lamp_ref[0:1, :] * lamp_ref[1:2, :], axis=-1, keepdims=True)
    b = jnp.sum(lamp_ref[2:3, :] * lamp_ref[3:4, :], axis=-1, keepdims=True)
    return jnp.exp(a) - jnp.exp(b) + lam_init


def _one_hot_cols(rel, width):
    lane = lax.broadcasted_iota(I32, rel.shape, 1)
    return jnp.concatenate([jnp.where(rel == lane + q * LANES, 1.0, 0.0) for q in range(width // LANES)], axis=1)


def _proj_body(x_ref, g_ref, w_ref, lng_ref, lnb_ref,
               q_ref, k_ref, v_ref, u_ref, vn_ref, sga_ref, sgs_ref, extra_ref, *, paged):
    tm, d = x_ref.shape
    h = _rms(x_ref[...], g_ref[...]).astype(BF16)

    def seg(s):
        return jnp.dot(h, w_ref[:, s * d:(s + 1) * d], preferred_element_type=F32)

    q_ref[...] = (seg(0) * Q_PRESCALE).astype(BF16)
    if paged:
        kk = seg(1)
        for pg in range(tm // PAGE_SIZE):
            k_ref[pg] = kk[pg * PAGE_SIZE:(pg + 1) * PAGE_SIZE, :].T
        vv = seg(2)
        for hd in range(N_HEADS):
            v_ref[:, hd, :] = vv[:, hd * V_DIM:(hd + 1) * V_DIM]
        extra_ref[...] = vv.astype(BF16)
    else:
        k_ref[...] = seg(1)
        v_ref[...] = seg(2)
    u_ref[...] = jax.nn.gelu(seg(3)).astype(BF16)
    vg = jax.nn.gelu(seg(4))
    xc = vg - jnp.mean(vg, axis=-1, keepdims=True)
    vn = xc * lax.rsqrt(jnp.mean(xc * xc, axis=-1, keepdims=True) + EPS) * lng_ref[...] + lnb_ref[...]
    vn_ref[...] = vn.astype(BF16)
    if not paged:
        extra_ref[...] = vn
    sga_ref[...] = jax.nn.sigmoid(seg(5)).astype(BF16)
    sgs_ref[...] = jax.nn.sigmoid(seg(6)).astype(BF16)


def _proj(x, gain, w_bf, ln_g, ln_b, *, tm, paged):
    n, d = x.shape
    row = pl.BlockSpec((tm, d), lambda i: (i, 0))
    vec = pl.BlockSpec((1, d), lambda i: (0, 0))
    rows_bf = jax.ShapeDtypeStruct((n, d), BF16)
    rows_f32 = jax.ShapeDtypeStruct((n, d), F32)
    if paged:
        k_spec = pl.BlockSpec((tm // PAGE_SIZE, d, PAGE_SIZE), lambda i: (i, 0, 0))
        k_shape = jax.ShapeDtypeStruct((n // PAGE_SIZE, d, PAGE_SIZE), F32)
        v_spec = pl.BlockSpec((tm, N_HEADS, V_DIM), lambda i: (i, 0, 0))
        v_shape = jax.ShapeDtypeStruct((n, N_HEADS, V_DIM), F32)
    else:
        k_spec, k_shape, v_spec, v_shape = row, rows_f32, row, rows_f32
    out_specs = [row, k_spec, v_spec, row, row, row, row, row]
    out_shape = [rows_bf, k_shape, v_shape, rows_bf, rows_bf, rows_bf, rows_bf, rows_bf if paged else rows_f32]
    return pl.pallas_call(
        functools.partial(_proj_body, paged=paged),
        grid=(n // tm,),
        in_specs=[row, vec, pl.BlockSpec((d, N_SEG * d), lambda i: (0, 0)), vec, vec],
        out_specs=out_specs,
        out_shape=out_shape,
        compiler_params=_cparams(1),
        name="proj",
    )(x, gain.reshape(1, d), w_bf, ln_g.reshape(1, d), ln_b.reshape(1, d))


def _attn_prompt_body(q_ref, k_ref, vb, sga_ref, gain_ref, lamp_ref, o_ref,
                      kb, qs, acc, m_s, l_s, *, tq, lam_init):
    i = pl.program_id(2)

    @pl.when(i == 0)
    def _():
        for pg in range(k_ref.shape[0]):
            kb[:, pg * PAGE_SIZE:(pg + 1) * PAGE_SIZE] = k_ref[pg].astype(BF16)

    q = q_ref[...]
    lane = lax.broadcasted_iota(I32, q.shape, 1)
    zero = jnp.zeros_like(q)
    qs[0:tq, :] = jnp.where(lane < HEAD_DIM, q, zero)
    qs[tq:, :] = jnp.where(lane >= HEAD_DIM, q, zero)
    m_s[...] = jnp.full(m_s.shape, NEG, F32)
    l_s[...] = jnp.zeros(l_s.shape, F32)
    acc[...] = jnp.zeros(acc.shape, F32)

    def step(start, width, masked):
        s = jnp.dot(qs[...], kb[:, pl.ds(start, width)], preferred_element_type=F32)
        if masked:
            r = lax.broadcasted_iota(I32, s.shape, 0)
            r = jnp.where(r >= tq, r - tq, r)
            c = lax.broadcasted_iota(I32, s.shape, 1)
            s = jnp.where(c <= r, s, NEG)
        m_prev = m_s[...]
        m_new = jnp.maximum(m_prev, jnp.max(s, axis=-1, keepdims=True))
        alpha = jnp.exp2(m_prev - m_new)
        p = jnp.exp2(s - jnp.tile(m_new, (1, width // LANES)))
        l_s[...] = alpha * l_s[...] + jnp.sum(p, axis=-1, keepdims=True)
        acc[...] = alpha * acc[...] + jnp.dot(p.astype(BF16), vb[pl.ds(start, width), :],
                                              preferred_element_type=F32)
        m_s[...] = m_new

    def off_diag_pair(jj, carry):
        step(pl.multiple_of(jj * 2 * tq, 2 * tq), 2 * tq, False)
        return carry

    lax.fori_loop(0, i // 2, off_diag_pair, 0)

    @pl.when(i % 2 == 1)
    def _():
        step(pl.multiple_of((i - 1) * tq, tq), tq, False)

    step(pl.multiple_of(i * tq, tq), tq, True)

    o = acc[...] / l_s[...]
    lam = _lam_value(lamp_ref, lam_init)
    od = o[0:tq, :] - lam * o[tq:, :]
    y = _rms(od, gain_ref[...]) * (1.0 - lam_init)
    o_ref[...] = (sga_ref[...].astype(F32) * y).astype(BF16)


def _attn_prompt(q, kt, v, sga, gain, lamp, *, lam_init, tq):
    b, t, d = q.shape
    assert V_DIM == LANES
    qspec = pl.BlockSpec((None, tq, V_DIM), lambda bi, h, i: (bi, i, h))
    kspec = pl.BlockSpec((None, t // PAGE_SIZE, V_DIM, PAGE_SIZE), lambda bi, h, i: (bi, 0, h, 0))
    vspec = pl.BlockSpec((None, t, V_DIM), lambda bi, h, i: (bi, 0, h))
    return pl.pallas_call(
        functools.partial(_attn_prompt_body, tq=tq, lam_init=lam_init),
        grid=(b, N_HEADS, t // tq),
        in_specs=[qspec, kspec, vspec, qspec,
                  pl.BlockSpec((None, 1, V_DIM), lambda bi, h, i: (h, 0, 0)),
                  pl.BlockSpec(lamp.shape, lambda bi, h, i: (0, 0))],
        out_specs=qspec,
        out_shape=jax.ShapeDtypeStruct((b, t, d), BF16),
        scratch_shapes=[pltpu.VMEM((V_DIM, t), BF16),
                        pltpu.VMEM((2 * tq, V_DIM), BF16), pltpu.VMEM((2 * tq, V_DIM), F32),
                        pltpu.VMEM((2 * tq, LANES), F32), pltpu.VMEM((2 * tq, LANES), F32)],
        compiler_params=_cparams(3),
        name="attn_prompt",
    )(q, kt, v, sga, gain.reshape(N_HEADS, 1, V_DIM), lamp)


def _attn_sample_body(pt_ref, q_ref, kn_ref, vn_ref, sga_ref, gain_ref, lamp_ref, ck_ref, cv_ref, o_ref,
                      kbuf, vbuf, sem, *, layer, n_pages, s_len, lam_init):
    i = pl.program_id(0)
    n_seq = pl.num_programs(0)
    rows = N_HEADS * 2 * s_len
    d = q_ref.shape[1]

    def page_copies(seq, slot):
        out = []
        for p in range(n_pages):
            page = pt_ref[seq * n_pages + p]
            out.append(pltpu.make_async_copy(ck_ref.at[layer, page], kbuf.at[slot, p], sem.at[0, slot]))
            out.append(pltpu.make_async_copy(cv_ref.at[layer, page], vbuf.at[slot, p], sem.at[1, slot]))
        return out

    @pl.when(i == 0)
    def _():
        for cp in page_copies(0, 0):
            cp.start()

    @pl.when(i + 1 < n_seq)
    def _():
        for cp in page_copies(i + 1, (i + 1) % 2):
            cp.start()

    slot = i % 2
    for cp in page_copies(i, slot):
        cp.wait()

    qt = jnp.tile(q_ref[...], (N_HEADS * 2, 1))
    r = lax.broadcasted_iota(I32, (rows, d), 0)
    c = lax.broadcasted_iota(I32, (rows, d), 1)
    qbd = jnp.where((c // HEAD_DIM) == (r // s_len), qt, jnp.zeros_like(qt))

    s_past = [jnp.dot(qbd, kbuf[slot, p].astype(BF16), preferred_element_type=F32)
              for p in range(n_pages)]
    s_new = lax.dot_general(qbd, kn_ref[...].astype(BF16), NT_DIMS, preferred_element_type=F32)
    rn = lax.broadcasted_iota(I32, s_new.shape, 0) % s_len
    cn = lax.broadcasted_iota(I32, s_new.shape, 1)
    s_new = jnp.where(cn <= rn, s_new, NEG)

    m = jnp.max(s_new, axis=-1, keepdims=True)
    for sp in s_past:
        m = jnp.maximum(m, jnp.max(sp, axis=-1, keepdims=True))
    p_new = jnp.exp2(s_new - m)
    l = jnp.sum(p_new, axis=-1, keepdims=True)
    o = jnp.dot(p_new.astype(BF16), vn_ref[...].astype(BF16), preferred_element_type=F32)
    for p in range(n_pages):
        pp = jnp.exp2(s_past[p] - m)
        l = l + jnp.sum(pp, axis=-1, keepdims=True)
        v_page = jnp.concatenate([vbuf[slot, p, pl.ds(h, PAGE_SIZE, stride=N_HEADS), :] for h in range(N_HEADS)],
                                 axis=1)
        o = o + jnp.dot(pp.astype(BF16), v_page.astype(BF16), preferred_element_type=F32)
    o = o / l

    lam = _lam_value(lamp_ref, lam_init)
    for h in range(N_HEADS):
        cols = slice(h * V_DIM, (h + 1) * V_DIM)
        r0 = h * 2 * s_len
        od = o[r0:r0 + s_len, cols] - lam * o[r0 + s_len:r0 + 2 * s_len, cols]
        y = _rms(od, gain_ref[h:h + 1, :]) * (1.0 - lam_init)
        o_ref[:, cols] = (sga_ref[:, cols].astype(F32) * y).astype(BF16)


def _attn_sample(q, k_new, v_new, sga, cache_kt, cache_v, page_table, gain, lamp, *, layer, lam_init):
    n, s_len, d = q.shape
    n_pages = page_table.shape[1]
    row = pl.BlockSpec((None, s_len, d), lambda i, pt: (i, 0, 0))
    grid_spec = pltpu.PrefetchScalarGridSpec(
        num_scalar_prefetch=1,
        grid=(n,),
        in_specs=[row, row, row, row,
                  pl.BlockSpec(gain.shape, lambda i, pt: (0, 0)),
                  pl.BlockSpec(lamp.shape, lambda i, pt: (0, 0)),
                  pl.BlockSpec(memory_space=pl.ANY), pl.BlockSpec(memory_space=pl.ANY)],
        out_specs=row,
        scratch_shapes=[pltpu.VMEM((2, n_pages, d, PAGE_SIZE), F32),
                        pltpu.VMEM((2, n_pages, PAGE_SIZE * N_HEADS, V_DIM), F32),
                        pltpu.SemaphoreType.DMA((2, 2))],
    )
    return pl.pallas_call(
        functools.partial(_attn_sample_body, layer=layer, n_pages=n_pages, s_len=s_len, lam_init=lam_init),
        grid_spec=grid_spec,
        out_shape=jax.ShapeDtypeStruct((n, s_len, d), BF16),
        compiler_params=_cparams(1),
        name="attn_sample",
    )(page_table.reshape(-1), q, k_new, v_new, sga, gain, lamp, cache_kt, cache_v)


def _mix_body(ag_ref, u_ref, vn_ref, sgs_ref, x_ref, ws_ref, bt_ref, wout_ref, gffn_ref, *rest,
              with_router):
    if with_router:
        router_ref, xo_ref, h_ref, eidx_ref, gate_ref, mixed = rest
    else:
        xo_ref, h_ref, mixed = rest
    tm = x_ref.shape[0]
    rr = lax.broadcasted_iota(I32, (CHUNK, CHUNK), 0)
    cc = lax.broadcasted_iota(I32, (CHUNK, CHUNK), 1)
    tril = cc <= rr
    for g in range(N_GROUPS):
        wg = jnp.where(tril, ws_ref[g], 0.0).astype(BF16)
        bias = bt_ref[:, g:g + 1]
        cols = slice(g * GROUP_DIM, (g + 1) * GROUP_DIM)
        for ci in range(tm // CHUNK):
            rows = slice(ci * CHUNK, (ci + 1) * CHUNK)
            s = jnp.dot(wg, vn_ref[rows, cols], preferred_element_type=F32) + bias
            sgu = u_ref[rows, cols].astype(F32) * s
            mix = ag_ref[rows, cols].astype(F32) + sgs_ref[rows, cols].astype(F32) * sgu
            mixed[rows, cols] = mix.astype(BF16)
    x_new = x_ref[...] + jnp.dot(mixed[...], wout_ref[...], preferred_element_type=F32)
    xo_ref[...] = x_new
    h = _rms(x_new, gffn_ref[...])
    h_ref[...] = h.astype(BF16)
    if with_router:
        h_hi = h.astype(BF16)
        h_lo = (h - h_hi.astype(F32)).astype(BF16)
        a = lax.dot_general(router_ref[0], h_hi, NT_DIMS, preferred_element_type=F32)
        b = lax.dot_general(router_ref[1], h_lo, NT_DIMS, preferred_element_type=F32)
        lg = a[0:N_EXPERTS] + a[N_EXPERTS:] + b[0:N_EXPERTS]
        ex = lax.broadcasted_iota(I32, lg.shape, 0)
        m1 = jnp.max(lg, axis=0, keepdims=True)
        i1 = jnp.min(jnp.where(lg == m1, ex, N_EXPERTS), axis=0, keepdims=True)
        lg2 = jnp.where(ex == i1, -jnp.inf, lg)
        m2 = jnp.max(lg2, axis=0, keepdims=True)
        i2 = jnp.min(jnp.where(lg2 == m2, ex, N_EXPERTS), axis=0, keepdims=True)
        e = jnp.exp(m2 - m1)
        g1 = 1.0 / (1.0 + e)
        g2 = e / (1.0 + e)
        eidx_ref[...] = jnp.where(ex == 0, i1, jnp.where(ex == 1, i2, 0))
        gate_ref[...] = jnp.where(ex == 0, g1, jnp.where(ex == 1, g2, 0.0))


def _mix_out(ag, u, vn, sgs, x, ws, bt, wout_bf, g_ffn, router_pad, *, tm):
    n, d = x.shape
    with_router = router_pad is not None
    row = pl.BlockSpec((tm, d), lambda i: (i, 0))
    per_expert = pl.BlockSpec((N_EXPERTS, tm), lambda i: (0, i))

    def full(a):
        return pl.BlockSpec(a.shape, lambda i: (0,) * a.ndim)

    g2 = g_ffn.reshape(1, d)
    in_specs = [row, row, row, row, row, full(ws), full(bt), full(wout_bf), full(g2)]
    args = [ag, u, vn, sgs, x, ws, bt, wout_bf, g2]
    out_specs = [row, row]
    out_shape = [jax.ShapeDtypeStruct((n, d), F32), jax.ShapeDtypeStruct((n, d), BF16)]
    if with_router:
        in_specs.append(full(router_pad))
        args.append(router_pad)
        out_specs += [per_expert, per_expert]
        out_shape += [jax.ShapeDtypeStruct((N_EXPERTS, n), I32), jax.ShapeDtypeStruct((N_EXPERTS, n), F32)]
    return pl.pallas_call(
        functools.partial(_mix_body, with_router=with_router),
        grid=(n // tm,),
        in_specs=in_specs,
        out_specs=out_specs,
        out_shape=out_shape,
        scratch_shapes=[pltpu.VMEM((tm, d), BF16)],
        compiler_params=_cparams(1),
        name="mix_out",
    )(*args)


def _ffn_body(te_ref, nv_ref, x_ref, wg_ref, wu_ref, wd_ref, side_ref, o_ref, acc, *, routed):
    del te_ref
    i = pl.program_id(0)
    j = pl.program_id(1)
    last = pl.num_programs(1) - 1
    valid = i < nv_ref[0]

    @pl.when(valid)
    def _():
        h = x_ref[...]
        a = jax.nn.silu(jnp.dot(h, wg_ref[...], preferred_element_type=F32))
        a = (a * jnp.dot(h, wu_ref[...], preferred_element_type=F32)).astype(BF16)
        y = jnp.dot(a, wd_ref[...], preferred_element_type=F32)

        @pl.when(j == 0)
        def _():
            acc[...] = y

        @pl.when(j > 0)
        def _():
            acc[...] += y

        @pl.when(j == last)
        def _():
            if routed:
                o_ref[...] = (acc[...] * jnp.tile(side_ref[...], (1, acc.shape[1] // LANES))).astype(o_ref.dtype)
            else:
                o_ref[...] = side_ref[...] + acc[...]

    @pl.when(jnp.logical_and(jnp.logical_not(valid), j == last))
    def _():
        o_ref[...] = jnp.zeros(o_ref.shape, o_ref.dtype)


def _ffn(x, wg, wu, wd, tile_expert, n_valid, side, *, tm, tf, routed):
    n, d = x.shape
    f = wg.shape[2]

    def row_idx(i, j, te, nv):
        return (jnp.minimum(i, nv[0] - 1), 0)

    side_spec = pl.BlockSpec((tm, side.shape[1]), row_idx)
    grid_spec = pltpu.PrefetchScalarGridSpec(
        num_scalar_prefetch=2,
        grid=(n // tm, f // tf),
        in_specs=[pl.BlockSpec((tm, d), row_idx),
                  pl.BlockSpec((None, d, tf), lambda i, j, te, nv: (te[i], 0, j)),
                  pl.BlockSpec((None, d, tf), lambda i, j, te, nv: (te[i], 0, j)),
                  pl.BlockSpec((None, tf, d), lambda i, j, te, nv: (te[i], j, 0)),
                  side_spec],
        out_specs=pl.BlockSpec((tm, d), lambda i, j, te, nv: (i, 0)),
        scratch_shapes=[pltpu.VMEM((tm, d), F32)],
    )
    return pl.pallas_call(
        functools.partial(_ffn_body, routed=routed),
        grid_spec=grid_spec,
        out_shape=jax.ShapeDtypeStruct((n, d), BF16 if routed else F32),
        compiler_params=_cparams(2),
        name="ffn",
    )(tile_expert, n_valid, x, wg, wu, wd, side)


def _dispatch_body(tile_ref, chunk_ref, first_ref, valid_ref, src_ref, h_ref, o_ref, *, tc):
    del tile_ref
    s = pl.program_id(0)

    @pl.when(valid_ref[s] == 1)
    def _():
        rel = src_ref[...] - chunk_ref[s] * tc
        pick = _one_hot_cols(rel, tc).astype(BF16)
        rows = jnp.dot(pick, h_ref[...], preferred_element_type=F32).astype(BF16)

        @pl.when(first_ref[s] == 1)
        def _():
            o_ref[...] = rows

        @pl.when(first_ref[s] == 0)
        def _():
            o_ref[...] += rows


def _dispatch(h, src_rep, sched, *, tm, tc):
    n, d = h.shape
    n_slots = src_rep.shape[0]
    tile, chunk, first, valid = sched
    grid_spec = pltpu.PrefetchScalarGridSpec(
        num_scalar_prefetch=4,
        grid=(tile.shape[0],),
        in_specs=[pl.BlockSpec((tm, LANES), lambda s, t, c, f, v: (t[s], 0)),
                  pl.BlockSpec((tc, d), lambda s, t, c, f, v: (c[s], 0))],
        out_specs=pl.BlockSpec((tm, d), lambda s, t, c, f, v: (t[s], 0)),
    )
    return pl.pallas_call(
        functools.partial(_dispatch_body, tc=tc),
        grid_spec=grid_spec,
        out_shape=jax.ShapeDtypeStruct((n_slots, d), BF16),
        compiler_params=_cparams(1),
        name="dispatch",
    )(tile, chunk, first, valid, src_rep, h)


def _combine_body(chunk_ref, tile_ref, first_ref, last_ref, valid_ref,
                  s0_ref, s1_ref, ys_ref, x_ref, gfin_ref, o_ref, acc, *, tm, final_norm):
    del chunk_ref
    s = pl.program_id(0)

    @pl.when(first_ref[s] == 1)
    def _():
        acc[...] = jnp.zeros(acc.shape, F32)

    @pl.when(valid_ref[s] == 1)
    def _():
        base = tile_ref[s] * tm
        hit = _one_hot_cols(s0_ref[...] - base, tm) + _one_hot_cols(s1_ref[...] - base, tm)
        acc[...] += jnp.dot(hit.astype(BF16), ys_ref[...], preferred_element_type=F32)

    @pl.when(last_ref[s] == 1)
    def _():
        x = x_ref[...] + acc[...]
        o_ref[...] = _rms(x, gfin_ref[...]) if final_norm else x


def _combine(x, ys, slot0_rep, slot1_rep, g_final, sched, *, tm, tc, final_norm):
    n, d = x.shape
    chunk, tile, first, last, valid = sched
    tok = pl.BlockSpec((tc, LANES), lambda s, c, t, f, l, v: (c[s], 0))
    grid_spec = pltpu.PrefetchScalarGridSpec(
        num_scalar_prefetch=5,
        grid=(chunk.shape[0],),
        in_specs=[tok, tok,
                  pl.BlockSpec((tm, d), lambda s, c, t, f, l, v: (t[s], 0)),
                  pl.BlockSpec((tc, d), lambda s, c, t, f, l, v: (c[s], 0)),
                  pl.BlockSpec((1, d), lambda s, c, t, f, l, v: (0, 0))],
        out_specs=pl.BlockSpec((tc, d), lambda s, c, t, f, l, v: (c[s], 0)),
        scratch_shapes=[pltpu.VMEM((tc, d), F32)],
    )
    return pl.pallas_call(
        functools.partial(_combine_body, tm=tm, final_norm=final_norm),
        grid_spec=grid_spec,
        out_shape=jax.ShapeDtypeStruct((n, d), F32),
        compiler_params=_cparams(1),
        name="combine",
    )(chunk, tile, first, last, valid, slot0_rep, slot1_rep, ys, x, g_final.reshape(1, d))


def _flatten_ranges(lengths, n_steps):
    end = jnp.cumsum(lengths)
    total = end[-1]
    s = jnp.arange(n_steps, dtype=I32)
    sc = jnp.minimum(s, total - 1)
    seg = jnp.sum((end[None, :] <= sc[:, None]).astype(I32), axis=1)
    off = sc - (end - lengths)[seg]
    return seg, off, (s < total).astype(I32)


def _route(eidx, gates, *, tm, tc):
    n = eidx.shape[0]
    n_pairs = n * TOP_K
    n_chunks = n // tc
    n_tiles = n_pairs // tm + N_EXPERTS
    n_steps = n_tiles + N_EXPERTS * n_chunks
    experts = jnp.arange(N_EXPERTS, dtype=I32)

    flat_e = eidx.reshape(-1)
    onehot = (flat_e[:, None] == experts[None, :]).astype(I32)
    csum = jnp.cumsum(onehot, axis=0)
    rank = jnp.sum(csum * onehot, axis=1) - 1
    counts = csum[-1]
    tiles_per = (counts + tm - 1) // tm
    tile_end = jnp.cumsum(tiles_per)
    tile_start = tile_end - tiles_per
    offs = tile_start * tm
    slot = jnp.sum(onehot * offs[None, :], axis=1) + rank
    n_valid = tile_end[-1:]

    upd = jnp.stack([(jnp.arange(n_pairs, dtype=I32) // TOP_K).astype(F32), gates.reshape(-1)], axis=1)
    init = jnp.concatenate([jnp.full((n_tiles * tm, 1), -1.0, F32), jnp.zeros((n_tiles * tm, 1), F32)], axis=1)
    per_slot = init.at[slot].set(upd)
    src_rep = jnp.broadcast_to(per_slot[:, 0:1].astype(I32), (n_tiles * tm, LANES))
    gate_rep = jnp.broadcast_to(per_slot[:, 1:2], (n_tiles * tm, LANES))

    t = jnp.arange(n_tiles, dtype=I32)
    used = t < n_valid[0]
    te = jnp.minimum(jnp.sum((tile_end[None, :] <= t[:, None]).astype(I32), axis=1), N_EXPERTS - 1)
    tile_expert = jnp.where(used, te, te[n_valid[0] - 1])
    r_first = (t - tile_start[te]) * tm
    r_last = jnp.minimum(r_first + tm, counts[te]) - 1
    cs_t = csum[:, te]
    p_first = jnp.sum((cs_t <= r_first[None, :]).astype(I32), axis=0)
    p_last = jnp.sum((cs_t <= r_last[None, :]).astype(I32), axis=0)
    c_lo = (p_first // TOP_K) // tc
    c_hi = (p_last // TOP_K) // tc
    d_tile, d_off, d_valid = _flatten_ranges(jnp.where(used, c_hi - c_lo + 1, 1), n_steps)
    d_chunk = jnp.where(used[d_tile], c_lo[d_tile] + d_off, 0)
    d_first = d_valid * (d_off == 0).astype(I32)
    dispatch_sched = (d_tile, d_chunk, d_first, d_valid)

    before = jnp.concatenate([jnp.zeros((1, N_EXPERTS), I32), csum[TOP_K * tc - 1::TOP_K * tc]], axis=0)
    cnt = before[1:] - before[:-1]
    lo_slot = offs[None, :] + before[:-1]
    lo_tile = (lo_slot // tm).reshape(-1)
    hi_tile = ((lo_slot + cnt - 1) // tm).reshape(-1)
    span = jnp.where(cnt.reshape(-1) > 0, hi_tile - lo_tile + 1, 0)
    c_seg, c_off, c_valid = _flatten_ranges(span, n_steps)
    c_chunk = c_seg // N_EXPERTS
    c_tile = lo_tile[c_seg] + c_off
    prev_chunk = jnp.concatenate([jnp.full((1,), -1, I32), c_chunk[:-1]])
    next_chunk = jnp.concatenate([c_chunk[1:], jnp.full((1,), -1, I32)])
    next_valid = jnp.concatenate([c_valid[1:], jnp.zeros((1,), I32)])
    c_first = c_valid * (c_chunk != prev_chunk).astype(I32)
    c_last = c_valid * jnp.logical_or(c_chunk != next_chunk, next_valid == 0).astype(I32)
    combine_sched = (c_chunk, c_tile, c_first, c_last, c_valid)

    slot2 = slot.reshape(n, TOP_K)
    slot0_rep = jnp.broadcast_to(slot2[:, 0:1], (n, LANES))
    slot1_rep = jnp.broadcast_to(slot2[:, 1:2], (n, LANES))
    return src_rep, gate_rep, tile_expert, n_valid, dispatch_sched, slot0_rep, slot1_rep, combine_sched


def kernel(x_prompt, x_sample, cache_k, cache_v, page_table, norm_mix, w_in, lam_q1, lam_k1, lam_q2, lam_k2,
           attn_head_gain, sgu_ln_g, sgu_ln_b, sgu_w, sgu_b, w_out, norm_ffn, ffn_w_gate, ffn_w_up,
           ffn_w_down, moe_router, moe_w_gate, moe_w_up, moe_w_down, norm_final):
    depth = w_in.shape[0]
    b, t, d = x_prompt.shape
    n_s, s_len, _ = x_sample.shape
    n_p = b * t
    n_sr = n_s * s_len
    n_pool = cache_k.shape[1]
    assert depth % 2 == 0, "the last layer is expected to be a routed layer"
    assert CHUNK % s_len == 0

    n_pg = t // PAGE_SIZE
    cache_kt = jnp.swapaxes(cache_k.reshape(depth, n_pool, PAGE_SIZE, d), 2, 3)
    cache_vr = cache_v.reshape(depth, n_pool, PAGE_SIZE * N_HEADS, V_DIM)
    xp = x_prompt.reshape(n_p, d)
    xs = x_sample.reshape(n_sr, d)
    tm_p, tm_s = 512, 512

    k_p, v_p, k_s, v_s, sv_s = [], [], [], [], []
    for l in range(depth):
        lam_init = 0.8 - 0.6 * math.exp(-0.3 * l)
        lamp = jnp.stack([lam_q1[l], lam_k1[l], lam_q2[l], lam_k2[l]])
        w_in_bf = w_in[l].astype(BF16)
        w_out_bf = w_out[l].astype(BF16)
        gain = attn_head_gain[l]
        ws_p = sgu_w[l]
        bt_p = sgu_b[l].T
        reps = CHUNK // s_len
        eye = jnp.eye(reps, dtype=F32)
        ws_s = jax.vmap(lambda w: jnp.kron(eye, w))(sgu_w[l][:, :s_len, :s_len])
        bt_s = jnp.tile(sgu_b[l][:, :s_len].T, (reps, 1))
        routed = l % 2 == 1
        j = l // 2
        router_pad = None
        if routed:
            r_t = moe_router[j].T
            r_hi = r_t.astype(BF16)
            r_lo = (r_t - r_hi.astype(F32)).astype(BF16)
            router_pad = jnp.stack([jnp.concatenate([r_hi, r_lo]), jnp.concatenate([r_hi, jnp.zeros_like(r_hi)])])

        q, k, v, u, vn, sga, sgs, v_bf = _proj(xp, norm_mix[l], w_in_bf, sgu_ln_g[l], sgu_ln_b[l],
                                               tm=tm_p, paged=True)
        ag = _attn_prompt(q.reshape(b, t, d), k.reshape(b, n_pg, d, PAGE_SIZE), v_bf.reshape(b, t, d),
                          sga.reshape(b, t, d), gain, lamp, lam_init=lam_init, tq=512).reshape(n_p, d)
        mix_p = _mix_out(ag, u, vn, sgs, xp, ws_p, bt_p, w_out_bf, norm_ffn[l], router_pad, tm=tm_p)
        k_p.append(k)
        v_p.append(v)

        q, k, v, u, vn, sga, sgs, vnf = _proj(xs, norm_mix[l], w_in_bf, sgu_ln_g[l], sgu_ln_b[l],
                                              tm=tm_s, paged=False)
        ag = _attn_sample(q.reshape(n_s, s_len, d), k.reshape(n_s, s_len, d), v.reshape(n_s, s_len, d),
                          sga.reshape(n_s, s_len, d), cache_kt, cache_vr, page_table, gain, lamp,
                          layer=l, lam_init=lam_init).reshape(n_sr, d)
        mix_s = _mix_out(ag, u, vn, sgs, xs, ws_s, bt_s, w_out_bf, norm_ffn[l], router_pad, tm=tm_s)
        k_s.append(k)
        v_s.append(v)
        sv_s.append(vnf)

        if not routed:
            wg, wu, wd = (ffn_w_gate[j:j + 1].astype(BF16), ffn_w_up[j:j + 1].astype(BF16),
                          ffn_w_down[j:j + 1].astype(BF16))
            tf = wg.shape[2] // 2
            outs = []
            for (x_new, h), tm in ((mix_p, tm_p), (mix_s, tm_s)):
                nt = x_new.shape[0] // tm
                outs.append(_ffn(h, wg, wu, wd, jnp.zeros((nt,), I32), jnp.full((1,), nt, I32),
                                 x_new, tm=tm, tf=tf, routed=False))
            xp, xs = outs
        else:
            tm = tc = 512
            x_all = jnp.concatenate([mix_p[0], mix_s[0]])
            h_all = jnp.concatenate([mix_p[1], mix_s[1]])
            eidx = jnp.concatenate([mix_p[2][:TOP_K], mix_s[2][:TOP_K]], axis=1).T
            gates = jnp.concatenate([mix_p[3][:TOP_K], mix_s[3][:TOP_K]], axis=1).T
            (src_rep, gate_rep, tile_expert, n_valid, dispatch_sched,
             slot0_rep, slot1_rep, combine_sched) = _route(eidx, gates, tm=tm, tc=tc)
            hs = _dispatch(h_all, src_rep, dispatch_sched, tm=tm, tc=tc)
            ys = _ffn(hs, moe_w_gate[j].astype(BF16), moe_w_up[j].astype(BF16), moe_w_down[j].astype(BF16),
                      tile_expert, n_valid, gate_rep, tm=tm, tf=moe_w_gate.shape[3] // 2, routed=True)
            x_all = _combine(x_all, ys, slot0_rep, slot1_rep, norm_final, combine_sched,
                             tm=tm, tc=tc, final_norm=(l == depth - 1))
            xp, xs = x_all[:n_p], x_all[n_p:]

    y_prompt = xp.reshape(b, t, d)
    y_sample = xs.reshape(n_s, s_len, d)
    k_prompt = jnp.moveaxis(jnp.stack(k_p).reshape(depth, b, n_pg, N_HEADS, 2, HEAD_DIM, PAGE_SIZE), 6, 3)
    v_prompt = jnp.stack(v_p).reshape(depth, b, n_pg, PAGE_SIZE, N_HEADS, V_DIM)
    k_sample = jnp.stack(k_s).reshape(depth, n_s, s_len, N_HEADS, 2, HEAD_DIM)
    v_sample = jnp.stack(v_s).reshape(depth, n_s, s_len, N_HEADS, V_DIM)
    sgu_v_sample = jnp.stack(sv_s).reshape(depth, n_s, s_len, d)
    return (y_prompt, y_sample, k_prompt, v_prompt, k_sample, v_sample, sgu_v_sample)
```

```python
import functools
import math

import jax
import jax.numpy as jnp
from jax import lax
from jax.experimental import pallas as pl
from jax.experimental.pallas import tpu as pltpu

F32 = jnp.float32
BF16 = jnp.bfloat16
I32 = jnp.int32

N_HEADS = 8
HEAD_DIM = 64
V_DIM = 2 * HEAD_DIM
PAIR = 2 * V_DIM
CHUNK = 128
N_GROUPS = 8
GROUP_DIM = 128
N_EXPERTS = 8
TOP_K = 2
PAGE_SIZE = 128
EPS = 1e-6
NEG = -1e30
Q_PRESCALE = HEAD_DIM ** -0.5 * math.log2(math.e)
N_SEG = 7
LANES = 128
VMEM_LIMIT = 56 * 1024 * 1024

NT_DIMS = (((1,), (1,)), ((), ()))


def _cparams(n_axes, vmem=VMEM_LIMIT):
    return pltpu.CompilerParams(dimension_semantics=("arbitrary",) * n_axes,
                                vmem_limit_bytes=vmem)


def _rms(x, gain):
    return x * lax.rsqrt(jnp.mean(x * x, axis=-1, keepdims=True) + EPS) * gain


def _lam_value(lamp_ref, lam_init):
    a = jnp.sum(lamp_ref[0:1, :] * lamp_ref[1:2, :], axis=-1, keepdims=True)
    b = jnp.sum(lamp_ref[2:3, :] * lamp_ref[3:4, :], axis=-1, keepdims=True)
    return jnp.exp(a) - jnp.exp(b) + lam_init


def _one_hot_cols(rel, width):
    lane = lax.broadcasted_iota(I32, rel.shape, 1)
    return jnp.concatenate([jnp.where(rel == lane + q * LANES, 1.0, 0.0) for q in range(width // LANES)], axis=1)


def _proj_body(x_ref, g_ref, w_ref, lng_ref, lnb_ref,
               q_ref, k_ref, v_ref, u_ref, vn_ref, sga_ref, sgs_ref, extra_ref, *, paged):
    tm, d = x_ref.shape
    h = _rms(x_ref[...], g_ref[...]).astype(BF16)

    def seg(s):
        return jnp.dot(h, w_ref[:, s * d:(s + 1) * d], preferred_element_type=F32)

    q_ref[...] = (seg(0) * Q_PRESCALE).astype(BF16)
    if paged:
        kk = seg(1)
        for pg in range(tm // PAGE_SIZE):
            k_ref[pg] = kk[pg * PAGE_SIZE:(pg + 1) * PAGE_SIZE, :].T
        vv = seg(2)
        for hd in range(N_HEADS):
            v_ref[:, hd, :] = vv[:, hd * V_DIM:(hd + 1) * V_DIM]
        extra_ref[...] = vv.astype(BF16)
    else:
        k_ref[...] = seg(1)
        v_ref[...] = seg(2)
    u_ref[...] = jax.nn.gelu(seg(3)).astype(BF16)
    vg = jax.nn.gelu(seg(4))
    xc = vg - jnp.mean(vg, axis=-1, keepdims=True)
    vn = xc * lax.rsqrt(jnp.mean(xc * xc, axis=-1, keepdims=True) + EPS) * lng_ref[...] + lnb_ref[...]
    vn_ref[...] = vn.astype(BF16)
    if not paged:
        extra_ref[...] = vn
    sga_ref[...] = jax.nn.sigmoid(seg(5)).astype(BF16)
    sgs_ref[...] = jax.nn.sigmoid(seg(6)).astype(BF16)


def _proj(x, gain, w_bf, ln_g, ln_b, *, tm, paged):
    n, d = x.shape
    row = pl.BlockSpec((tm, d), lambda i: (i, 0))
    vec = pl.BlockSpec((1, d), lambda i: (0, 0))
    rows_bf = jax.ShapeDtypeStruct((n, d), BF16)
    rows_f32 = jax.ShapeDtypeStruct((n, d), F32)
    if paged:
        k_spec = pl.BlockSpec((tm // PAGE_SIZE, d, PAGE_SIZE), lambda i: (i, 0, 0))
        k_shape = jax.ShapeDtypeStruct((n // PAGE_SIZE, d, PAGE_SIZE), F32)
        v_spec = pl.BlockSpec((tm, N_HEADS, V_DIM), lambda i: (i, 0, 0))
        v_shape = jax.ShapeDtypeStruct((n, N_HEADS, V_DIM), F32)
    else:
        k_spec, k_shape, v_spec, v_shape = row, rows_f32, row, rows_f32
    out_specs = [row, k_spec, v_spec, row, row, row, row, row]
    out_shape = [rows_bf, k_shape, v_shape, rows_bf, rows_bf, rows_bf, rows_bf, rows_bf if paged else rows_f32]
    return pl.pallas_call(
        functools.partial(_proj_body, paged=paged),
        grid=(n // tm,),
        in_specs=[row, vec, pl.BlockSpec((d, N_SEG * d), lambda i: (0, 0)), vec, vec],
        out_specs=out_specs,
        out_shape=out_shape,
        compiler_params=_cparams(1),
        name="proj",
    )(x, gain.reshape(1, d), w_bf, ln_g.reshape(1, d), ln_b.reshape(1, d))


def _attn_prompt_body(q_ref, k_ref, vb, sga_ref, gain_ref, lamp_ref, o_ref,
                      kb, qs, acc, m_s, l_s, *, tq, lam_init):
    i = pl.program_id(2)

    @pl.when(i == 0)
    def _():
        for pg in range(k_ref.shape[0]):
            kb[:, pg * PAGE_SIZE:(pg + 1) * PAGE_SIZE] = k_ref[pg].astype(BF16)

    q = q_ref[...]
    lane = lax.broadcasted_iota(I32, q.shape, 1)
    zero = jnp.zeros_like(q)
    qs[0:tq, :] = jnp.where(lane < HEAD_DIM, q, zero)
    qs[tq:, :] = jnp.where(lane >= HEAD_DIM, q, zero)
    m_s[...] = jnp.full(m_s.shape, NEG, F32)
    l_s[...] = jnp.zeros(l_s.shape, F32)
    acc[...] = jnp.zeros(acc.shape, F32)

    def step(start, width, masked):
        s = jnp.dot(qs[...], kb[:, pl.ds(start, width)], preferred_element_type=F32)
        if masked:
            r = lax.broadcasted_iota(I32, s.shape, 0)
            r = jnp.where(r >= tq, r - tq, r)
            c = lax.broadcasted_iota(I32, s.shape, 1)
            s = jnp.where(c <= r, s, NEG)
        m_prev = m_s[...]
        m_new = jnp.maximum(m_prev, jnp.max(s, axis=-1, keepdims=True))
        alpha = jnp.exp2(m_prev - m_new)
        p = jnp.exp2(s - jnp.tile(m_new, (1, width // LANES)))
        l_s[...] = alpha * l_s[...] + jnp.sum(p, axis=-1, keepdims=True)
        acc[...] = alpha * acc[...] + jnp.dot(p.astype(BF16), vb[pl.ds(start, width), :],
                                              preferred_element_type=F32)
        m_s[...] = m_new

    def off_diag_pair(jj, carry):
        step(pl.multiple_of(jj * 2 * tq, 2 * tq), 2 * tq, False)
        return carry

    lax.fori_loop(0, i // 2, off_diag_pair, 0)

    @pl.when(i % 2 == 1)
    def _():
        step(pl.multiple_of((i - 1) * tq, tq), tq, False)

    step(pl.multiple_of(i * tq, tq), tq, True)

    o = acc[...] / l_s[...]
    lam = _lam_value(lamp_ref, lam_init)
    od = o[0:tq, :] - lam * o[tq:, :]
    y = _rms(od, gain_ref[...]) * (1.0 - lam_init)
    o_ref[...] = (sga_ref[...].astype(F32) * y).astype(BF16)


def _attn_prompt(q, kt, v, sga, gain, lamp, *, lam_init, tq):
    b, t, d = q.shape
    assert V_DIM == LANES
    qspec = pl.BlockSpec((None, tq, V_DIM), lambda bi, h, i: (bi, i, h))
    kspec = pl.BlockSpec((None, t // PAGE_SIZE, V_DIM, PAGE_SIZE), lambda bi, h, i: (bi, 0, h, 0))
    vspec = pl.BlockSpec((None, t, V_DIM), lambda bi, h, i: (bi, 0, h))
    return pl.pallas_call(
        functools.partial(_attn_prompt_body, tq=tq, lam_init=lam_init),
        grid=(b, N_HEADS, t // tq),
        in_specs=[qspec, kspec, vspec, qspec,
                  pl.BlockSpec((None, 1, V_DIM), lambda bi, h, i: (h, 0, 0)),
                  pl.BlockSpec(lamp.shape, lambda bi, h, i: (0, 0))],
        out_specs=qspec,
        out_shape=jax.ShapeDtypeStruct((b, t, d), BF16),
        scratch_shapes=[pltpu.VMEM((V_DIM, t), BF16),
                        pltpu.VMEM((2 * tq, V_DIM), BF16), pltpu.VMEM((2 * tq, V_DIM), F32),
                        pltpu.VMEM((2 * tq, LANES), F32), pltpu.VMEM((2 * tq, LANES), F32)],
        compiler_params=_cparams(3),
        name="attn_prompt",
    )(q, kt, v, sga, gain.reshape(N_HEADS, 1, V_DIM), lamp)


def _attn_sample_body(pt_ref, q_ref, kn_ref, vn_ref, sga_ref, gain_ref, lamp_ref, ck_ref, cv_ref, o_ref,
                      kbuf, vbuf, sem, *, layer, n_pages, s_len, lam_init):
    i = pl.program_id(0)
    n_seq = pl.num_programs(0)

    def page_copies(seq, slot):
        out = []
        for p in range(n_pages):
            page = pt_ref[seq * n_pages + p]
            out.append(pltpu.make_async_copy(ck_ref.at[layer, page], kbuf.at[slot, p], sem.at[0, slot]))
            out.append(pltpu.make_async_copy(cv_ref.at[layer, page], vbuf.at[slot, p], sem.at[1, slot]))
        return out

    @pl.when(i == 0)
    def _():
        for cp in page_copies(0, 0):
            cp.start()

    @pl.when(i + 1 < n_seq)
    def _():
        for cp in page_copies(i + 1, (i + 1) % 2):
            cp.start()

    slot = i % 2
    for cp in page_copies(i, slot):
        cp.wait()

    n_pairs = N_HEADS // 2
    prow = 4 * s_len
    r = lax.broadcasted_iota(I32, (prow, PAIR), 0)
    c = lax.broadcasted_iota(I32, (prow, PAIR), 1)
    on_diag = (c // HEAD_DIM) == (r // s_len)
    qp = []
    for hp in range(n_pairs):
        qt = jnp.tile(q_ref[:, hp * PAIR:(hp + 1) * PAIR], (4, 1))
        qp.append(jnp.where(on_diag, qt, jnp.zeros_like(qt)))

    def pair_rows(fn):
        return jnp.concatenate([fn(hp) for hp in range(n_pairs)], axis=0)

    s_past = [pair_rows(lambda hp: jnp.dot(qp[hp], kbuf[slot, p, hp * PAIR:(hp + 1) * PAIR, :].astype(BF16),
                                           preferred_element_type=F32))
              for p in range(n_pages)]
    s_new = pair_rows(lambda hp: lax.dot_general(qp[hp], kn_ref[:, hp * PAIR:(hp + 1) * PAIR].astype(BF16),
                                                 NT_DIMS, preferred_element_type=F32))
    rn = lax.broadcasted_iota(I32, s_new.shape, 0) % s_len
    cn = lax.broadcasted_iota(I32, s_new.shape, 1)
    s_new = jnp.where(cn <= rn, s_new, NEG)

    m = jnp.max(s_new, axis=-1, keepdims=True)
    for sp in s_past:
        m = jnp.maximum(m, jnp.max(sp, axis=-1, keepdims=True))
    p_new = jnp.exp2(s_new - m)
    l = jnp.sum(p_new, axis=-1, keepdims=True)
    p_new = p_new.astype(BF16)
    o = [jnp.dot(p_new[hp * prow:(hp + 1) * prow], vn_ref[:, hp * PAIR:(hp + 1) * PAIR].astype(BF16),
                 preferred_element_type=F32) for hp in range(n_pairs)]
    for p in range(n_pages):
        pp = jnp.exp2(s_past[p] - m)
        l = l + jnp.sum(pp, axis=-1, keepdims=True)
        pp = pp.astype(BF16)
        for hp in range(n_pairs):
            v_pair = jnp.concatenate([vbuf[slot, p, pl.ds(2 * hp + hl, PAGE_SIZE, stride=N_HEADS), :]
                                      for hl in range(2)], axis=1)
            o[hp] = o[hp] + jnp.dot(pp[hp * prow:(hp + 1) * prow], v_pair.astype(BF16),
                                    preferred_element_type=F32)

    lam = _lam_value(lamp_ref, lam_init)
    for h in range(N_HEADS):
        hp, hl = divmod(h, 2)
        cols = slice(h * V_DIM, (h + 1) * V_DIM)
        r0 = hl * 2 * s_len
        oh = o[hp][r0:r0 + 2 * s_len, hl * V_DIM:(hl + 1) * V_DIM] / l[h * 2 * s_len:(h + 1) * 2 * s_len]
        od = oh[0:s_len] - lam * oh[s_len:2 * s_len]
        y = _rms(od, gain_ref[h:h + 1, :]) * (1.0 - lam_init)
        o_ref[:, cols] = (sga_ref[:, cols].astype(F32) * y).astype(BF16)


def _attn_sample(q, k_new, v_new, sga, cache_kt, cache_v, page_table, gain, lamp, *, layer, lam_init):
    n, s_len, d = q.shape
    n_pages = page_table.shape[1]
    row = pl.BlockSpec((None, s_len, d), lambda i, pt: (i, 0, 0))
    grid_spec = pltpu.PrefetchScalarGridSpec(
        num_scalar_prefetch=1,
        grid=(n,),
        in_specs=[row, row, row, row,
                  pl.BlockSpec(gain.shape, lambda i, pt: (0, 0)),
                  pl.BlockSpec(lamp.shape, lambda i, pt: (0, 0)),
                  pl.BlockSpec(memory_space=pl.ANY), pl.BlockSpec(memory_space=pl.ANY)],
        out_specs=row,
        scratch_shapes=[pltpu.VMEM((2, n_pages, d, PAGE_SIZE), F32),
                        pltpu.VMEM((2, n_pages, PAGE_SIZE * N_HEADS, V_DIM), F32),
                        pltpu.SemaphoreType.DMA((2, 2))],
    )
    return pl.pallas_call(
        functools.partial(_attn_sample_body, layer=layer, n_pages=n_pages, s_len=s_len, lam_init=lam_init),
        grid_spec=grid_spec,
        out_shape=jax.ShapeDtypeStruct((n, s_len, d), BF16),
        compiler_params=_cparams(1),
        name="attn_sample",
    )(page_table.reshape(-1), q, k_new, v_new, sga, gain, lamp, cache_kt, cache_v)


def _mix_body(ag_ref, u_ref, vn_ref, sgs_ref, x_ref, ws_ref, bt_ref, wout_ref, gffn_ref, *rest,
              with_router):
    if with_router:
        router_ref, xo_ref, h_ref, eidx_ref, gate_ref, mixed = rest
    else:
        xo_ref, h_ref, mixed = rest
    tm = x_ref.shape[0]
    rr = lax.broadcasted_iota(I32, (CHUNK, CHUNK), 0)
    cc = lax.broadcasted_iota(I32, (CHUNK, CHUNK), 1)
    tril = cc <= rr
    for g in range(N_GROUPS):
        wg = jnp.where(tril, ws_ref[g], 0.0).astype(BF16)
        bias = bt_ref[:, g:g + 1]
        cols = slice(g * GROUP_DIM, (g + 1) * GROUP_DIM)
        for ci in range(tm // CHUNK):
            rows = slice(ci * CHUNK, (ci + 1) * CHUNK)
            s = jnp.dot(wg, vn_ref[rows, cols], preferred_element_type=F32) + bias
            sgu = u_ref[rows, cols].astype(F32) * s
            mix = ag_ref[rows, cols].astype(F32) + sgs_ref[rows, cols].astype(F32) * sgu
            mixed[rows, cols] = mix.astype(BF16)
    x_new = x_ref[...] + jnp.dot(mixed[...], wout_ref[...], preferred_element_type=F32)
    xo_ref[...] = x_new
    h = _rms(x_new, gffn_ref[...])
    h_ref[...] = h.astype(BF16)
    if with_router:
        h_hi = h.astype(BF16)
        h_lo = (h - h_hi.astype(F32)).astype(BF16)
        a = lax.dot_general(router_ref[0], h_hi, NT_DIMS, preferred_element_type=F32)
        b = lax.dot_general(router_ref[1], h_lo, NT_DIMS, preferred_element_type=F32)
        lg = a[0:N_EXPERTS] + a[N_EXPERTS:] + b[0:N_EXPERTS]
        ex = lax.broadcasted_iota(I32, lg.shape, 0)
        m1 = jnp.max(lg, axis=0, keepdims=True)
        i1 = jnp.min(jnp.where(lg == m1, ex, N_EXPERTS), axis=0, keepdims=True)
        lg2 = jnp.where(ex == i1, -jnp.inf, lg)
        m2 = jnp.max(lg2, axis=0, keepdims=True)
        i2 = jnp.min(jnp.where(lg2 == m2, ex, N_EXPERTS), axis=0, keepdims=True)
        e = jnp.exp(m2 - m1)
        g1 = 1.0 / (1.0 + e)
        g2 = e / (1.0 + e)
        eidx_ref[...] = jnp.where(ex == 0, i1, jnp.where(ex == 1, i2, 0))
        gate_ref[...] = jnp.where(ex == 0, g1, jnp.where(ex == 1, g2, 0.0))


def _mix_out(ag, u, vn, sgs, x, ws, bt, wout_bf, g_ffn, router_pad, *, tm):
    n, d = x.shape
    with_router = router_pad is not None
    row = pl.BlockSpec((tm, d), lambda i: (i, 0))
    per_expert = pl.BlockSpec((N_EXPERTS, tm), lambda i: (0, i))

    def full(a):
        return pl.BlockSpec(a.shape, lambda i: (0,) * a.ndim)

    g2 = g_ffn.reshape(1, d)
    in_specs = [row, row, row, row, row, full(ws), full(bt), full(wout_bf), full(g2)]
    args = [ag, u, vn, sgs, x, ws, bt, wout_bf, g2]
    out_specs = [row, row]
    out_shape = [jax.ShapeDtypeStruct((n, d), F32), jax.ShapeDtypeStruct((n, d), BF16)]
    if with_router:
        in_specs.append(full(router_pad))
        args.append(router_pad)
        out_specs += [per_expert, per_expert]
        out_shape += [jax.ShapeDtypeStruct((N_EXPERTS, n), I32), jax.ShapeDtypeStruct((N_EXPERTS, n), F32)]
    return pl.pallas_call(
        functools.partial(_mix_body, with_router=with_router),
        grid=(n // tm,),
        in_specs=in_specs,
        out_specs=out_specs,
        out_shape=out_shape,
        scratch_shapes=[pltpu.VMEM((tm, d), BF16)],
        compiler_params=_cparams(1),
        name="mix_out",
    )(*args)


def _ffn_body(te_ref, nv_ref, x_ref, wg_ref, wu_ref, wd_ref, side_ref, o_ref, acc, *, routed):
    del te_ref
    i = pl.program_id(0)
    j = pl.program_id(1)
    last = pl.num_programs(1) - 1
    valid = i < nv_ref[0]

    @pl.when(valid)
    def _():
        h = x_ref[...]
        a = jax.nn.silu(jnp.dot(h, wg_ref[...], preferred_element_type=F32))
        a = (a * jnp.dot(h, wu_ref[...], preferred_element_type=F32)).astype(BF16)
        y = jnp.dot(a, wd_ref[...], preferred_element_type=F32)

        @pl.when(j == 0)
        def _():
            acc[...] = y

        @pl.when(j > 0)
        def _():
            acc[...] += y

        @pl.when(j == last)
        def _():
            if routed:
                o_ref[...] = (acc[...] * jnp.tile(side_ref[...], (1, acc.shape[1] // LANES))).astype(o_ref.dtype)
            else:
                o_ref[...] = side_ref[...] + acc[...]

    @pl.when(jnp.logical_and(jnp.logical_not(valid), j == last))
    def _():
        o_ref[...] = jnp.zeros(o_ref.shape, o_ref.dtype)


def _ffn(x, wg, wu, wd, tile_expert, n_valid, side, *, tm, tf, routed):
    n, d = x.shape
    f = wg.shape[2]

    def row_idx(i, j, te, nv):
        return (jnp.minimum(i, nv[0] - 1), 0)

    side_spec = pl.BlockSpec((tm, side.shape[1]), row_idx)
    grid_spec = pltpu.PrefetchScalarGridSpec(
        num_scalar_prefetch=2,
        grid=(n // tm, f // tf),
        in_specs=[pl.BlockSpec((tm, d), row_idx),
                  pl.BlockSpec((None, d, tf), lambda i, j, te, nv: (te[i], 0, j)),
                  pl.BlockSpec((None, d, tf), lambda i, j, te, nv: (te[i], 0, j)),
                  pl.BlockSpec((None, tf, d), lambda i, j, te, nv: (te[i], j, 0)),
                  side_spec],
        out_specs=pl.BlockSpec((tm, d), lambda i, j, te, nv: (i, 0)),
        scratch_shapes=[pltpu.VMEM((tm, d), F32)],
    )
    return pl.pallas_call(
        functools.partial(_ffn_body, routed=routed),
        grid_spec=grid_spec,
        out_shape=jax.ShapeDtypeStruct((n, d), BF16 if routed else F32),
        compiler_params=_cparams(2),
        name="ffn",
    )(tile_expert, n_valid, x, wg, wu, wd, side)


def _dispatch_body(tile_ref, chunk_ref, first_ref, valid_ref, src_ref, hp_ref, hs_ref, o_ref, *, tc, cp):
    del tile_ref
    s = pl.program_id(0)

    @pl.when(valid_ref[s] == 1)
    def _():
        chunk = chunk_ref[s]
        rel = src_ref[...] - chunk * tc
        pick = _one_hot_cols(rel, tc).astype(BF16)
        h = jnp.where(chunk < cp, hp_ref[...], hs_ref[...])
        rows = jnp.dot(pick, h, preferred_element_type=F32).astype(BF16)

        @pl.when(first_ref[s] == 1)
        def _():
            o_ref[...] = rows

        @pl.when(first_ref[s] == 0)
        def _():
            o_ref[...] += rows


def _group_specs(tc, width, cp, chunk_of):
    return (pl.BlockSpec((tc, width), lambda s, *pre: (jnp.minimum(chunk_of(pre)[s], cp - 1), 0)),
            pl.BlockSpec((tc, width), lambda s, *pre: (jnp.maximum(chunk_of(pre)[s] - cp, 0), 0)))


def _dispatch(h_p, h_s, src_rep, sched, *, tm, tc):
    d = h_p.shape[1]
    cp = h_p.shape[0] // tc
    n_slots = src_rep.shape[0]
    tile, chunk, first, valid = sched
    hp_spec, hs_spec = _group_specs(tc, d, cp, lambda pre: pre[1])
    grid_spec = pltpu.PrefetchScalarGridSpec(
        num_scalar_prefetch=4,
        grid=(tile.shape[0],),
        in_specs=[pl.BlockSpec((tm, LANES), lambda s, t, c, f, v: (t[s], 0)), hp_spec, hs_spec],
        out_specs=pl.BlockSpec((tm, d), lambda s, t, c, f, v: (t[s], 0)),
    )
    return pl.pallas_call(
        functools.partial(_dispatch_body, tc=tc, cp=cp),
        grid_spec=grid_spec,
        out_shape=jax.ShapeDtypeStruct((n_slots, d), BF16),
        compiler_params=_cparams(1),
        name="dispatch",
    )(tile, chunk, first, valid, src_rep, h_p, h_s)


def _combine_body(chunk_ref, tile_ref, first_ref, last_ref, valid_ref,
                  s0_ref, s1_ref, ys_ref, xp_ref, xs_ref, gfin_ref, op_ref, os_ref, acc, *, tm, cp, final_norm):
    s = pl.program_id(0)

    @pl.when(first_ref[s] == 1)
    def _():
        acc[...] = jnp.zeros(acc.shape, F32)

    @pl.when(valid_ref[s] == 1)
    def _():
        base = tile_ref[s] * tm
        hit = _one_hot_cols(s0_ref[...] - base, tm) + _one_hot_cols(s1_ref[...] - base, tm)
        acc[...] += jnp.dot(hit.astype(BF16), ys_ref[...], preferred_element_type=F32)

    def finish(x_ref, o_ref):
        x = x_ref[...] + acc[...]
        o_ref[...] = _rms(x, gfin_ref[...]) if final_norm else x

    is_last = last_ref[s] == 1
    in_prompt = chunk_ref[s] < cp

    @pl.when(jnp.logical_and(is_last, in_prompt))
    def _():
        finish(xp_ref, op_ref)

    @pl.when(jnp.logical_and(is_last, jnp.logical_not(in_prompt)))
    def _():
        finish(xs_ref, os_ref)


def _combine(x_p, x_s, ys, slot0_rep, slot1_rep, g_final, sched, *, tm, tc, final_norm):
    d = x_p.shape[1]
    cp = x_p.shape[0] // tc
    chunk, tile, first, last, valid = sched
    tok = pl.BlockSpec((tc, LANES), lambda s, c, t, f, l, v: (c[s], 0))
    xp_spec, xs_spec = _group_specs(tc, d, cp, lambda pre: pre[0])
    grid_spec = pltpu.PrefetchScalarGridSpec(
        num_scalar_prefetch=5,
        grid=(chunk.shape[0],),
        in_specs=[tok, tok,
                  pl.BlockSpec((tm, d), lambda s, c, t, f, l, v: (t[s], 0)),
                  xp_spec, xs_spec,
                  pl.BlockSpec((1, d), lambda s, c, t, f, l, v: (0, 0))],
        out_specs=[xp_spec, xs_spec],
        scratch_shapes=[pltpu.VMEM((tc, d), F32)],
    )
    return pl.pallas_call(
        functools.partial(_combine_body, tm=tm, cp=cp, final_norm=final_norm),
        grid_spec=grid_spec,
        out_shape=[jax.ShapeDtypeStruct(x_p.shape, F32), jax.ShapeDtypeStruct(x_s.shape, F32)],
        compiler_params=_cparams(1),
        name="combine",
    )(chunk, tile, first, last, valid, slot0_rep, slot1_rep, ys, x_p, x_s, g_final.reshape(1, d))


def _flatten_ranges(lengths, n_steps):
    end = jnp.cumsum(lengths)
    total = end[-1]
    s = jnp.arange(n_steps, dtype=I32)
    sc = jnp.minimum(s, total - 1)
    seg = jnp.sum((end[None, :] <= sc[:, None]).astype(I32), axis=1)
    off = sc - (end - lengths)[seg]
    return seg, off, (s < total).astype(I32)


def _route(eidx, gates, *, tm, tc):
    n = eidx.shape[0]
    n_pairs = n * TOP_K
    n_chunks = n // tc
    n_tiles = n_pairs // tm + N_EXPERTS
    n_steps = n_tiles + N_EXPERTS * n_chunks
    experts = jnp.arange(N_EXPERTS, dtype=I32)

    flat_e = eidx.reshape(-1)
    onehot = (flat_e[:, None] == experts[None, :]).astype(I32)
    csum = jnp.cumsum(onehot, axis=0)
    rank = jnp.sum(csum * onehot, axis=1) - 1
    counts = csum[-1]
    tiles_per = (counts + tm - 1) // tm
    tile_end = jnp.cumsum(tiles_per)
    tile_start = tile_end - tiles_per
    offs = tile_start * tm
    slot = jnp.sum(onehot * offs[None, :], axis=1) + rank
    n_valid = tile_end[-1:]

    upd = jnp.stack([(jnp.arange(n_pairs, dtype=I32) // TOP_K).astype(F32), gates.reshape(-1)], axis=1)
    init = jnp.concatenate([jnp.full((n_tiles * tm, 1), -1.0, F32), jnp.zeros((n_tiles * tm, 1), F32)], axis=1)
    per_slot = init.at[slot].set(upd)
    src_rep = jnp.broadcast_to(per_slot[:, 0:1].astype(I32), (n_tiles * tm, LANES))
    gate_rep = jnp.broadcast_to(per_slot[:, 1:2], (n_tiles * tm, LANES))

    t = jnp.arange(n_tiles, dtype=I32)
    used = t < n_valid[0]
    te = jnp.minimum(jnp.sum((tile_end[None, :] <= t[:, None]).astype(I32), axis=1), N_EXPERTS - 1)
    tile_expert = jnp.where(used, te, te[n_valid[0] - 1])
    r_first = (t - tile_start[te]) * tm
    r_last = jnp.minimum(r_first + tm, counts[te]) - 1
    cs_t = csum[:, te]
    p_first = jnp.sum((cs_t <= r_first[None, :]).astype(I32), axis=0)
    p_last = jnp.sum((cs_t <= r_last[None, :]).astype(I32), axis=0)
    c_lo = (p_first // TOP_K) // tc
    c_hi = (p_last // TOP_K) // tc
    d_tile, d_off, d_valid = _flatten_ranges(jnp.where(used, c_hi - c_lo + 1, 1), n_steps)
    d_chunk = jnp.where(used[d_tile], c_lo[d_tile] + d_off, 0)
    d_first = d_valid * (d_off == 0).astype(I32)
    dispatch_sched = (d_tile, d_chunk, d_first, d_valid)

    before = jnp.concatenate([jnp.zeros((1, N_EXPERTS), I32), csum[TOP_K * tc - 1::TOP_K * tc]], axis=0)
    cnt = before[1:] - before[:-1]
    lo_slot = offs[None, :] + before[:-1]
    lo_tile = (lo_slot // tm).reshape(-1)
    hi_tile = ((lo_slot + cnt - 1) // tm).reshape(-1)
    span = jnp.where(cnt.reshape(-1) > 0, hi_tile - lo_tile + 1, 0)
    c_seg, c_off, c_valid = _flatten_ranges(span, n_steps)
    c_chunk = c_seg // N_EXPERTS
    c_tile = lo_tile[c_seg] + c_off
    prev_chunk = jnp.concatenate([jnp.full((1,), -1, I32), c_chunk[:-1]])
    next_chunk = jnp.concatenate([c_chunk[1:], jnp.full((1,), -1, I32)])
    next_valid = jnp.concatenate([c_valid[1:], jnp.zeros((1,), I32)])
    c_first = c_valid * (c_chunk != prev_chunk).astype(I32)
    c_last = c_valid * jnp.logical_or(c_chunk != next_chunk, next_valid == 0).astype(I32)
    combine_sched = (c_chunk, c_tile, c_first, c_last, c_valid)

    slot2 = slot.reshape(n, TOP_K)
    slot0_rep = jnp.broadcast_to(slot2[:, 0:1], (n, LANES))
    slot1_rep = jnp.broadcast_to(slot2[:, 1:2], (n, LANES))
    return src_rep, gate_rep, tile_expert, n_valid, dispatch_sched, slot0_rep, slot1_rep, combine_sched


def kernel(x_prompt, x_sample, cache_k, cache_v, page_table, norm_mix, w_in, lam_q1, lam_k1, lam_q2, lam_k2,
           attn_head_gain, sgu_ln_g, sgu_ln_b, sgu_w, sgu_b, w_out, norm_ffn, ffn_w_gate, ffn_w_up,
           ffn_w_down, moe_router, moe_w_gate, moe_w_up, moe_w_down, norm_final):
    depth = w_in.shape[0]
    b, t, d = x_prompt.shape
    n_s, s_len, _ = x_sample.shape
    n_p = b * t
    n_sr = n_s * s_len
    n_pool = cache_k.shape[1]
    assert depth % 2 == 0, "the last layer is expected to be a routed layer"
    assert CHUNK % s_len == 0

    n_pg = t // PAGE_SIZE
    cache_kt = jnp.swapaxes(cache_k.reshape(depth, n_pool, PAGE_SIZE, d), 2, 3)
    cache_vr = cache_v.reshape(depth, n_pool, PAGE_SIZE * N_HEADS, V_DIM)
    xp = x_prompt.reshape(n_p, d)
    xs = x_sample.reshape(n_sr, d)
    tm_p, tm_s = 512, 512

    k_p, v_p, k_s, v_s, sv_s = [], [], [], [], []
    for l in range(depth):
        lam_init = 0.8 - 0.6 * math.exp(-0.3 * l)
        lamp = jnp.stack([lam_q1[l], lam_k1[l], lam_q2[l], lam_k2[l]])
        w_in_bf = w_in[l].astype(BF16)
        w_out_bf = w_out[l].astype(BF16)
        gain = attn_head_gain[l]
        ws_p = sgu_w[l]
        bt_p = sgu_b[l].T
        reps = CHUNK // s_len
        eye = jnp.eye(reps, dtype=F32)
        ws_s = jax.vmap(lambda w: jnp.kron(eye, w))(sgu_w[l][:, :s_len, :s_len])
        bt_s = jnp.tile(sgu_b[l][:, :s_len].T, (reps, 1))
        routed = l % 2 == 1
        j = l // 2
        router_pad = None
        if routed:
            r_t = moe_router[j].T
            r_hi = r_t.astype(BF16)
            r_lo = (r_t - r_hi.astype(F32)).astype(BF16)
            router_pad = jnp.stack([jnp.concatenate([r_hi, r_lo]), jnp.concatenate([r_hi, jnp.zeros_like(r_hi)])])

        q, k, v, u, vn, sga, sgs, v_bf = _proj(xp, norm_mix[l], w_in_bf, sgu_ln_g[l], sgu_ln_b[l],
                                               tm=tm_p, paged=True)
        ag = _attn_prompt(q.reshape(b, t, d), k.reshape(b, n_pg, d, PAGE_SIZE), v_bf.reshape(b, t, d),
                          sga.reshape(b, t, d), gain, lamp, lam_init=lam_init, tq=512).reshape(n_p, d)
        mix_p = _mix_out(ag, u, vn, sgs, xp, ws_p, bt_p, w_out_bf, norm_ffn[l], router_pad, tm=tm_p)
        k_p.append(k)
        v_p.append(v)

        q, k, v, u, vn, sga, sgs, vnf = _proj(xs, norm_mix[l], w_in_bf, sgu_ln_g[l], sgu_ln_b[l],
                                              tm=tm_s, paged=False)
        ag = _attn_sample(q.reshape(n_s, s_len, d), k.reshape(n_s, s_len, d), v.reshape(n_s, s_len, d),
                          sga.reshape(n_s, s_len, d), cache_kt, cache_vr, page_table, gain, lamp,
                          layer=l, lam_init=lam_init).reshape(n_sr, d)
        mix_s = _mix_out(ag, u, vn, sgs, xs, ws_s, bt_s, w_out_bf, norm_ffn[l], router_pad, tm=tm_s)
        k_s.append(k)
        v_s.append(v)
        sv_s.append(vnf)

        if not routed:
            wg, wu, wd = (ffn_w_gate[j:j + 1].astype(BF16), ffn_w_up[j:j + 1].astype(BF16),
                          ffn_w_down[j:j + 1].astype(BF16))
            tf = wg.shape[2] // 2
            outs = []
            for (x_new, h), tm in ((mix_p, tm_p), (mix_s, tm_s)):
                nt = x_new.shape[0] // tm
                outs.append(_ffn(h, wg, wu, wd, jnp.zeros((nt,), I32), jnp.full((1,), nt, I32),
                                 x_new, tm=tm, tf=tf, routed=False))
            xp, xs = outs
        else:
            tm = tc = 512
            assert n_p % tc == 0 and n_sr % tc == 0
            eidx =jnp.concatenate([mix_p[2][:TOP_K], mix_s[2][:TOP_K]], axis=1).T
            gates = jnp.concatenate([mix_p[3][:TOP_K], mix_s[3][:TOP_K]], axis=1).T
            (src_rep, gate_rep, tile_expert, n_valid, dispatch_sched,
             slot0_rep, slot1_rep, combine_sched) = _route(eidx, gates, tm=tm, tc=tc)
            hs = _dispatch(mix_p[1], mix_s[1], src_rep, dispatch_sched, tm=tm, tc=tc)
            ys = _ffn(hs, moe_w_gate[j].astype(BF16), moe_w_up[j].astype(BF16), moe_w_down[j].astype(BF16),
                      tile_expert, n_valid, gate_rep, tm=tm, tf=moe_w_gate.shape[3] // 2, routed=True)
            xp, xs = _combine(mix_p[0], mix_s[0], ys, slot0_rep, slot1_rep, norm_final, combine_sched,
                              tm=tm, tc=tc, final_norm=(l == depth - 1))

    y_prompt = xp.reshape(b, t, d)
    y_sample = xs.reshape(n_s, s_len, d)
    k_prompt = jnp.moveaxis(jnp.stack(k_p).reshape(depth, b, n_pg, N_HEADS, 2, HEAD_DIM, PAGE_SIZE), 6, 3)
    v_prompt = jnp.stack(v_p).reshape(depth, b, n_pg, PAGE_SIZE, N_HEADS, V_DIM)
    k_sample = jnp.stack(k_s).reshape(depth, n_s, s_len, N_HEADS, 2, HEAD_DIM)
    v_sample = jnp.stack(v_s).reshape(depth, n_s, s_len, N_HEADS, V_DIM)
    sgu_v_sample = jnp.stack(sv_s).reshape(depth, n_s, s_len, d)
    return (y_prompt, y_sample, k_prompt, v_prompt, k_sample, v_sample, sgu_v_sample)
```

```python
import functools
import math

import jax
import jax.numpy as jnp
from jax import lax
from jax.experimental import pallas as pl
from jax.experimental.pallas import tpu as pltpu

F32 = jnp.float32
BF16 = jnp.bfloat16
I32 = jnp.int32

N_HEADS = 8
HEAD_DIM = 64
V_DIM = 2 * HEAD_DIM
PAIR = 2 * V_DIM
CHUNK = 128
N_GROUPS = 8
GROUP_DIM = 128
N_EXPERTS = 8
TOP_K = 2
PAGE_SIZE = 128
EPS = 1e-6
NEG = -1e30
Q_PRESCALE = HEAD_DIM ** -0.5 * math.log2(math.e)
N_SEG = 7
LANES = 128
VMEM_LIMIT = 56 * 1024 * 1024

NT_DIMS = (((1,), (1,)), ((), ()))


def _cparams(n_axes, vmem=VMEM_LIMIT):
    return pltpu.CompilerParams(dimension_semantics=("arbitrary",) * n_axes,
                                vmem_limit_bytes=vmem)


def _rms(x, gain):
    return x * lax.rsqrt(jnp.mean(x * x, axis=-1, keepdims=True) + EPS) * gain


def _lam_value(lamp_ref, lam_init):
    a = jnp.sum(lamp_ref[0:1, :] * lamp_ref[1:2, :], axis=-1, keepdims=True)
    b = jnp.sum(lamp_ref[2:3, :] * lamp_ref[3:4, :], axis=-1, keepdims=True)
    return jnp.exp(a) - jnp.exp(b) + lam_init


def _one_hot_cols(rel, width):
    lane = lax.broadcasted_iota(I32, rel.shape, 1)
    return jnp.concatenate([jnp.where(rel == lane + q * LANES, 1.0, 0.0) for q in range(width // LANES)], axis=1)


def _proj_body(x_ref, g_ref, w_ref, lng_ref, lnb_ref,
               q_ref, k_ref, v_ref, u_ref, vn_ref, sga_ref, sgs_ref, extra_ref, *, paged):
    tm, d = x_ref.shape
    h = _rms(x_ref[...], g_ref[...]).astype(BF16)

    def seg(s):
        return jnp.dot(h, w_ref[:, s * d:(s + 1) * d], preferred_element_type=F32)

    q_ref[...] = (seg(0) * Q_PRESCALE).astype(BF16)
    if paged:
        kk = seg(1)
        for pg in range(tm // PAGE_SIZE):
            k_ref[pg] = kk[pg * PAGE_SIZE:(pg + 1) * PAGE_SIZE, :].T
        vv = seg(2)
        for hd in range(N_HEADS):
            v_ref[:, hd, :] = vv[:, hd * V_DIM:(hd + 1) * V_DIM]
        extra_ref[...] = vv.astype(BF16)
    else:
        k_ref[...] = seg(1)
        v_ref[...] = seg(2)
    u_ref[...] = jax.nn.gelu(seg(3)).astype(BF16)
    vg = jax.nn.gelu(seg(4))
    xc = vg - jnp.mean(vg, axis=-1, keepdims=True)
    vn = xc * lax.rsqrt(jnp.mean(xc * xc, axis=-1, keepdims=True) + EPS) * lng_ref[...] + lnb_ref[...]
    vn_ref[...] = vn.astype(BF16)
    if not paged:
        extra_ref[...] = vn
    sga_ref[...] = jax.nn.sigmoid(seg(5)).astype(BF16)
    sgs_ref[...] = jax.nn.sigmoid(seg(6)).astype(BF16)


def _proj(x, gain, w_bf, ln_g, ln_b, *, tm, paged):
    n, d = x.shape
    row = pl.BlockSpec((tm, d), lambda i: (i, 0))
    vec = pl.BlockSpec((1, d), lambda i: (0, 0))
    rows_bf = jax.ShapeDtypeStruct((n, d), BF16)
    rows_f32 = jax.ShapeDtypeStruct((n, d), F32)
    if paged:
        k_spec = pl.BlockSpec((tm // PAGE_SIZE, d, PAGE_SIZE), lambda i: (i, 0, 0))
        k_shape = jax.ShapeDtypeStruct((n // PAGE_SIZE, d, PAGE_SIZE), F32)
        v_spec = pl.BlockSpec((tm, N_HEADS, V_DIM), lambda i: (i, 0, 0))
        v_shape = jax.ShapeDtypeStruct((n, N_HEADS, V_DIM), F32)
    else:
        k_spec, k_shape, v_spec, v_shape = row, rows_f32, row, rows_f32
    out_specs = [row, k_spec, v_spec, row, row, row, row, row]
    out_shape = [rows_bf, k_shape, v_shape, rows_bf, rows_bf, rows_bf, rows_bf, rows_bf if paged else rows_f32]
    return pl.pallas_call(
        functools.partial(_proj_body, paged=paged),
        grid=(n // tm,),
        in_specs=[row, vec, pl.BlockSpec((d, N_SEG * d), lambda i: (0, 0)), vec, vec],
        out_specs=out_specs,
        out_shape=out_shape,
        compiler_params=_cparams(1),
        name="proj",
    )(x, gain.reshape(1, d), w_bf, ln_g.reshape(1, d), ln_b.reshape(1, d))


def _prompt_attn_tile(i, q, sga, gain_ref, lamp_ref, kb, vb, qs, acc, m_s, l_s, *, tq, lam_init):
    lane = lax.broadcasted_iota(I32, q.shape, 1)
    zero = jnp.zeros_like(q)
    qs[0:tq, :] = jnp.where(lane < HEAD_DIM, q, zero)
    qs[tq:, :] = jnp.where(lane >= HEAD_DIM, q, zero)
    m_s[...] = jnp.full(m_s.shape, NEG, F32)
    l_s[...] = jnp.zeros(l_s.shape, F32)
    acc[...] = jnp.zeros(acc.shape, F32)

    def step(start, width, masked):
        s = jnp.dot(qs[...], kb[:, pl.ds(start, width)], preferred_element_type=F32)
        if masked:
            r = lax.broadcasted_iota(I32, s.shape, 0)
            r = jnp.where(r >= tq, r - tq, r)
            c = lax.broadcasted_iota(I32, s.shape, 1)
            s = jnp.where(c <= r, s, NEG)
        m_prev = m_s[...]
        m_new = jnp.maximum(m_prev, jnp.max(s, axis=-1, keepdims=True))
        alpha = jnp.exp2(m_prev - m_new)
        p = jnp.exp2(s - jnp.tile(m_new, (1, width // LANES)))
        l_s[...] = alpha * l_s[...] + jnp.sum(p, axis=-1, keepdims=True)
        acc[...] = alpha * acc[...] + jnp.dot(p.astype(BF16), vb[pl.ds(start, width), :],
                                              preferred_element_type=F32)
        m_s[...] = m_new

    def off_diag_pair(jj, carry):
        step(pl.multiple_of(jj * 2 * tq, 2 * tq), 2 * tq, False)
        return carry

    lax.fori_loop(0, i // 2, off_diag_pair, 0)

    @pl.when(i % 2 == 1)
    def _():
        step(pl.multiple_of((i - 1) * tq, tq), tq, False)

    step(pl.multiple_of(i * tq, tq), tq, True)

    o = acc[...] / l_s[...]
    lam = _lam_value(lamp_ref, lam_init)
    od = o[0:tq, :] - lam * o[tq:, :]
    y = _rms(od, gain_ref[...]) * (1.0 - lam_init)
    return (sga.astype(F32) * y).astype(BF16)


def _sample_attn_seq(slot, q_ref, kn_ref, vn_ref, sga_ref, gain_ref, lamp_ref, kbuf, vbuf, o_ref,
                     *, n_pages, s_len, lam_init):
    n_pairs = N_HEADS // 2
    prow = 4 * s_len
    r = lax.broadcasted_iota(I32, (prow, PAIR), 0)
    c = lax.broadcasted_iota(I32, (prow, PAIR), 1)
    on_diag = (c // HEAD_DIM) == (r // s_len)
    qp = []
    for hp in range(n_pairs):
        qt = jnp.tile(q_ref[:, hp * PAIR:(hp + 1) * PAIR], (4, 1))
        qp.append(jnp.where(on_diag, qt, jnp.zeros_like(qt)))

    def pair_rows(fn):
        return jnp.concatenate([fn(hp) for hp in range(n_pairs)], axis=0)

    s_past = [pair_rows(lambda hp: jnp.dot(qp[hp], kbuf[slot, p, hp * PAIR:(hp + 1) * PAIR, :].astype(BF16),
                                           preferred_element_type=F32))
              for p in range(n_pages)]
    s_new = pair_rows(lambda hp: lax.dot_general(qp[hp], kn_ref[:, hp * PAIR:(hp + 1) * PAIR].astype(BF16),
                                                 NT_DIMS, preferred_element_type=F32))
    rn = lax.broadcasted_iota(I32, s_new.shape, 0) % s_len
    cn = lax.broadcasted_iota(I32, s_new.shape, 1)
    s_new = jnp.where(cn <= rn, s_new, NEG)

    m = jnp.max(s_new, axis=-1, keepdims=True)
    for sp in s_past:
        m = jnp.maximum(m, jnp.max(sp, axis=-1, keepdims=True))
    p_new = jnp.exp2(s_new - m)
    l = jnp.sum(p_new, axis=-1, keepdims=True)
    p_new = p_new.astype(BF16)
    o = [jnp.dot(p_new[hp * prow:(hp + 1) * prow], vn_ref[:, hp * PAIR:(hp + 1) * PAIR].astype(BF16),
                 preferred_element_type=F32) for hp in range(n_pairs)]
    for p in range(n_pages):
        pp = jnp.exp2(s_past[p] - m)
        l = l + jnp.sum(pp, axis=-1, keepdims=True)
        pp = pp.astype(BF16)
        for hp in range(n_pairs):
            v_pair = jnp.concatenate([vbuf[slot, p, pl.ds(2 * hp + hl, PAGE_SIZE, stride=N_HEADS), :]
                                      for hl in range(2)], axis=1)
            o[hp] = o[hp] + jnp.dot(pp[hp * prow:(hp + 1) * prow], v_pair.astype(BF16),
                                    preferred_element_type=F32)

    lam = _lam_value(lamp_ref, lam_init)
    for h in range(N_HEADS):
        hp, hl = divmod(h, 2)
        cols = slice(h * V_DIM, (h + 1) * V_DIM)
        r0 = hl * 2 * s_len
        oh = o[hp][r0:r0 + 2 * s_len, hl * V_DIM:(hl + 1) * V_DIM] / l[h * 2 * s_len:(h + 1) * 2 * s_len]
        od = oh[0:s_len] - lam * oh[s_len:2 * s_len]
        y = _rms(od, gain_ref[h:h + 1, :]) * (1.0 - lam_init)
        o_ref[:, cols] = (sga_ref[:, cols].astype(F32) * y).astype(BF16)


def _attn_body(pt_ref, qp_ref, kt_ref, vb_ref, sgap_ref, gainp_ref, lamp_ref,
               qs_ref, kn_ref, vn_ref, sgas_ref, gains_ref, ck_ref, cv_ref, op_ref, os_ref,
               kb, qst, acc, m_s, l_s, kbuf, vbuf, sem,
               *, layer, n_pages, s_len, tq, tiles, steps_per_head, lam_init):
    n = pl.program_id(0)
    n_seq = pl.num_programs(0)

    def page_copies(seq, slot):
        out = []
        for p in range(n_pages):
            page = pt_ref[seq * n_pages + p]
            out.append(pltpu.make_async_copy(ck_ref.at[layer, page], kbuf.at[slot, p], sem.at[0, slot]))
            out.append(pltpu.make_async_copy(cv_ref.at[layer, page], vbuf.at[slot, p], sem.at[1, slot]))
        return out

    @pl.when(n == 0)
    def _():
        for cp in page_copies(0, 0):
            cp.start()

    @pl.when(n + 1 < n_seq)
    def _():
        for cp in page_copies(n + 1, (n + 1) % 2):
            cp.start()

    part = n % steps_per_head

    @pl.when(part == 0)
    def _():
        for pg in range(kt_ref.shape[0]):
            kb[:, pg * PAGE_SIZE:(pg + 1) * PAGE_SIZE] = kt_ref[pg].astype(BF16)

    for t in range(tiles):
        rows = slice(t * tq, (t + 1) * tq)
        op_ref[rows, :] = _prompt_attn_tile(part * tiles + t, qp_ref[rows, :], sgap_ref[rows, :], gainp_ref,
                                            lamp_ref, kb, vb_ref, qst, acc, m_s, l_s, tq=tq, lam_init=lam_init)

    slot = n % 2
    for cp in page_copies(n, slot):
        cp.wait()
    _sample_attn_seq(slot, qs_ref, kn_ref, vn_ref, sgas_ref, gains_ref, lamp_ref, kbuf, vbuf, os_ref,
                     n_pages=n_pages, s_len=s_len, lam_init=lam_init)


def _attn(q_p, kt_p, v_p, sga_p, q_s, k_new, v_new, sga_s, cache_kt, cache_v, page_table, gain, lamp,
          *, layer, lam_init, tq):
    b, t, d = q_p.shape
    n_seq, s_len, _ = q_s.shape
    n_pages = page_table.shape[1]
    assert V_DIM == LANES
    nq = t // tq
    tiles = (b * N_HEADS * nq) // n_seq
    assert tiles * n_seq == b * N_HEADS * nq and nq % tiles == 0
    steps_per_head = nq // tiles

    def bh(n):
        return n // (steps_per_head * N_HEADS), (n // steps_per_head) % N_HEADS

    qspec = pl.BlockSpec((None, tiles * tq, V_DIM), lambda n, pt: (bh(n)[0], n % steps_per_head, bh(n)[1]))
    kspec = pl.BlockSpec((None, t // PAGE_SIZE, V_DIM, PAGE_SIZE), lambda n, pt: (bh(n)[0], 0, bh(n)[1], 0))
    vspec = pl.BlockSpec((None, t, V_DIM), lambda n, pt: (bh(n)[0], 0, bh(n)[1]))
    row = pl.BlockSpec((None, s_len, d), lambda n, pt: (n, 0, 0))
    grid_spec = pltpu.PrefetchScalarGridSpec(
        num_scalar_prefetch=1,
        grid=(n_seq,),
        in_specs=[qspec, kspec, vspec, qspec,
                  pl.BlockSpec((None, 1, V_DIM), lambda n, pt: (bh(n)[1], 0, 0)),
                  pl.BlockSpec(lamp.shape, lambda n, pt: (0, 0)),
                  row, row, row, row,
                  pl.BlockSpec(gain.shape, lambda n, pt: (0, 0)),
                  pl.BlockSpec(memory_space=pl.ANY), pl.BlockSpec(memory_space=pl.ANY)],
        out_specs=[qspec, row],
        scratch_shapes=[pltpu.VMEM((V_DIM, t), BF16),
                        pltpu.VMEM((2 * tq, V_DIM), BF16), pltpu.VMEM((2 * tq, V_DIM), F32),
                        pltpu.VMEM((2 * tq, LANES), F32), pltpu.VMEM((2 * tq, LANES), F32),
                        pltpu.VMEM((2, n_pages, d, PAGE_SIZE), F32),
                        pltpu.VMEM((2, n_pages, PAGE_SIZE * N_HEADS, V_DIM), F32),
                        pltpu.SemaphoreType.DMA((2, 2))],
    )
    return pl.pallas_call(
        functools.partial(_attn_body, layer=layer, n_pages=n_pages, s_len=s_len, tq=tq, tiles=tiles,
                          steps_per_head=steps_per_head, lam_init=lam_init),
        grid_spec=grid_spec,
        out_shape=[jax.ShapeDtypeStruct((b, t, d), BF16), jax.ShapeDtypeStruct((n_seq, s_len, d), BF16)],
        compiler_params=_cparams(1),
        name="attn",
    )(page_table.reshape(-1), q_p, kt_p, v_p, sga_p, gain.reshape(N_HEADS, 1, V_DIM), lamp,
      q_s, k_new, v_new, sga_s, gain, cache_kt, cache_v)


def _mix_body(ag_ref, u_ref, vn_ref, sgs_ref, x_ref, ws_ref, bt_ref, wout_ref, gffn_ref, *rest,
              with_router):
    if with_router:
        router_ref, xo_ref, h_ref, eidx_ref, gate_ref, mixed = rest
    else:
        xo_ref, h_ref, mixed = rest
    tm = x_ref.shape[0]
    rr = lax.broadcasted_iota(I32, (CHUNK, CHUNK), 0)
    cc = lax.broadcasted_iota(I32, (CHUNK, CHUNK), 1)
    tril = cc <= rr
    for g in range(N_GROUPS):
        wg = jnp.where(tril, ws_ref[g], 0.0).astype(BF16)
        bias = bt_ref[:, g:g + 1]
        cols = slice(g * GROUP_DIM, (g + 1) * GROUP_DIM)
        for ci in range(tm // CHUNK):
            rows = slice(ci * CHUNK, (ci + 1) * CHUNK)
            s = jnp.dot(wg, vn_ref[rows, cols], preferred_element_type=F32) + bias
            sgu = u_ref[rows, cols].astype(F32) * s
            mix = ag_ref[rows, cols].astype(F32) + sgs_ref[rows, cols].astype(F32) * sgu
            mixed[rows, cols] = mix.astype(BF16)
    x_new = x_ref[...] + jnp.dot(mixed[...], wout_ref[...], preferred_element_type=F32)
    xo_ref[...] = x_new
    h = _rms(x_new, gffn_ref[...])
    h_ref[...] = h.astype(BF16)
    if with_router:
        h_hi = h.astype(BF16)
        h_lo = (h - h_hi.astype(F32)).astype(BF16)
        a = lax.dot_general(router_ref[0], h_hi, NT_DIMS, preferred_element_type=F32)
        b = lax.dot_general(router_ref[1], h_lo, NT_DIMS, preferred_element_type=F32)
        lg = a[0:N_EXPERTS] + a[N_EXPERTS:] + b[0:N_EXPERTS]
        ex = lax.broadcasted_iota(I32, lg.shape, 0)
        m1 = jnp.max(lg, axis=0, keepdims=True)
        i1 = jnp.min(jnp.where(lg == m1, ex, N_EXPERTS), axis=0, keepdims=True)
        lg2 = jnp.where(ex == i1, -jnp.inf, lg)
        m2 = jnp.max(lg2, axis=0, keepdims=True)
        i2 = jnp.min(jnp.where(lg2 == m2, ex, N_EXPERTS), axis=0, keepdims=True)
        e = jnp.exp(m2 - m1)
        g1 = 1.0 / (1.0 + e)
        g2 = e / (1.0 + e)
        eidx_ref[...] = jnp.where(ex == 0, i1, jnp.where(ex == 1, i2, 0))
        gate_ref[...] = jnp.where(ex == 0, g1, jnp.where(ex == 1, g2, 0.0))


def _mix_out(ag, u, vn, sgs, x, ws, bt, wout_bf, g_ffn, router_pad, *, tm):
    n, d = x.shape
    with_router = router_pad is not None
    row = pl.BlockSpec((tm, d), lambda i: (i, 0))
    per_expert = pl.BlockSpec((N_EXPERTS, tm), lambda i: (0, i))

    def full(a):
        return pl.BlockSpec(a.shape, lambda i: (0,) * a.ndim)

    g2 = g_ffn.reshape(1, d)
    in_specs = [row, row, row, row, row, full(ws), full(bt), full(wout_bf), full(g2)]
    args = [ag, u, vn, sgs, x, ws, bt, wout_bf, g2]
    out_specs = [row, row]
    out_shape = [jax.ShapeDtypeStruct((n, d), F32), jax.ShapeDtypeStruct((n, d), BF16)]
    if with_router:
        in_specs.append(full(router_pad))
        args.append(router_pad)
        out_specs += [per_expert, per_expert]
        out_shape += [jax.ShapeDtypeStruct((N_EXPERTS, n), I32), jax.ShapeDtypeStruct((N_EXPERTS, n), F32)]
    return pl.pallas_call(
        functools.partial(_mix_body, with_router=with_router),
        grid=(n // tm,),
        in_specs=in_specs,
        out_specs=out_specs,
        out_shape=out_shape,
        scratch_shapes=[pltpu.VMEM((tm, d), BF16)],
        compiler_params=_cparams(1),
        name="mix_out",
    )(*args)


def _ffn_body(te_ref, nv_ref, x_ref, wg_ref, wu_ref, wd_ref, side_ref, o_ref, acc, *, routed):
    del te_ref
    i = pl.program_id(0)
    j = pl.program_id(1)
    last = pl.num_programs(1) - 1
    valid = i < nv_ref[0]

    @pl.when(valid)
    def _():
        h = x_ref[...]
        a = jax.nn.silu(jnp.dot(h, wg_ref[...], preferred_element_type=F32))
        a = (a * jnp.dot(h, wu_ref[...], preferred_element_type=F32)).astype(BF16)
        y = jnp.dot(a, wd_ref[...], preferred_element_type=F32)

        @pl.when(j == 0)
        def _():
            acc[...] = y

        @pl.when(j > 0)
        def _():
            acc[...] += y

        @pl.when(j == last)
        def _():
            if routed:
                o_ref[...] = (acc[...] * jnp.tile(side_ref[...], (1, acc.shape[1] // LANES))).astype(o_ref.dtype)
            else:
                o_ref[...] = side_ref[...] + acc[...]

    @pl.when(jnp.logical_and(jnp.logical_not(valid), j == last))
    def _():
        o_ref[...] = jnp.zeros(o_ref.shape, o_ref.dtype)


def _ffn(x, wg, wu, wd, tile_expert, n_valid, side, *, tm, tf, routed):
    n, d = x.shape
    f = wg.shape[2]

    def row_idx(i, j, te, nv):
        return (jnp.minimum(i, nv[0] - 1), 0)

    side_spec = pl.BlockSpec((tm, side.shape[1]), row_idx)
    grid_spec = pltpu.PrefetchScalarGridSpec(
        num_scalar_prefetch=2,
        grid=(n // tm, f // tf),
        in_specs=[pl.BlockSpec((tm, d), row_idx),
                  pl.BlockSpec((None, d, tf), lambda i, j, te, nv: (te[i], 0, j)),
                  pl.BlockSpec((None, d, tf), lambda i, j, te, nv: (te[i], 0, j)),
                  pl.BlockSpec((None, tf, d), lambda i, j, te, nv: (te[i], j, 0)),
                  side_spec],
        out_specs=pl.BlockSpec((tm, d), lambda i, j, te, nv: (i, 0)),
        scratch_shapes=[pltpu.VMEM((tm, d), F32)],
    )
    return pl.pallas_call(
        functools.partial(_ffn_body, routed=routed),
        grid_spec=grid_spec,
        out_shape=jax.ShapeDtypeStruct((n, d), BF16 if routed else F32),
        compiler_params=_cparams(2),
        name="ffn",
    )(tile_expert, n_valid, x, wg, wu, wd, side)


def _dispatch_body(tile_ref, chunk_ref, first_ref, valid_ref, src_ref, hp_ref, hs_ref, o_ref, *, tc, cp):
    del tile_ref
    s = pl.program_id(0)

    @pl.when(valid_ref[s] == 1)
    def _():
        chunk = chunk_ref[s]
        rel = src_ref[...] - chunk * tc
        pick = _one_hot_cols(rel, tc).astype(BF16)
        h = jnp.where(chunk < cp, hp_ref[...], hs_ref[...])
        rows = jnp.dot(pick, h, preferred_element_type=F32).astype(BF16)

        @pl.when(first_ref[s] == 1)
        def _():
            o_ref[...] = rows

        @pl.when(first_ref[s] == 0)
        def _():
            o_ref[...] += rows


def _group_specs(tc, width, cp, chunk_of):
    return (pl.BlockSpec((tc, width), lambda s, *pre: (jnp.minimum(chunk_of(pre)[s], cp - 1), 0)),
            pl.BlockSpec((tc, width), lambda s, *pre: (jnp.maximum(chunk_of(pre)[s] - cp, 0), 0)))


def _dispatch(h_p, h_s, src_rep, sched, *, tm, tc):
    d = h_p.shape[1]
    cp = h_p.shape[0] // tc
    n_slots = src_rep.shape[0]
    tile, chunk, first, valid = sched
    hp_spec, hs_spec = _group_specs(tc, d, cp, lambda pre: pre[1])
    grid_spec = pltpu.PrefetchScalarGridSpec(
        num_scalar_prefetch=4,
        grid=(tile.shape[0],),
        in_specs=[pl.BlockSpec((tm, LANES), lambda s, t, c, f, v: (t[s], 0)), hp_spec, hs_spec],
        out_specs=pl.BlockSpec((tm, d), lambda s, t, c, f, v: (t[s], 0)),
    )
    return pl.pallas_call(
        functools.partial(_dispatch_body, tc=tc, cp=cp),
        grid_spec=grid_spec,
        out_shape=jax.ShapeDtypeStruct((n_slots, d), BF16),
        compiler_params=_cparams(1),
        name="dispatch",
    )(tile, chunk, first, valid, src_rep, h_p, h_s)


def _combine_body(chunk_ref, tile_ref, first_ref, last_ref, valid_ref,
                  s0_ref, s1_ref, ys_ref, xp_ref, xs_ref, gfin_ref, op_ref, os_ref, acc, *, tm, cp, final_norm):
    s = pl.program_id(0)

    @pl.when(first_ref[s] == 1)
    def _():
        acc[...] = jnp.zeros(acc.shape, F32)

    @pl.when(valid_ref[s] == 1)
    def _():
        base = tile_ref[s] * tm
        hit = _one_hot_cols(s0_ref[...] - base, tm) + _one_hot_cols(s1_ref[...] - base, tm)
        acc[...] += jnp.dot(hit.astype(BF16), ys_ref[...], preferred_element_type=F32)

    def finish(x_ref, o_ref):
        x = x_ref[...] + acc[...]
        o_ref[...] = _rms(x, gfin_ref[...]) if final_norm else x

    is_last = last_ref[s] == 1
    in_prompt = chunk_ref[s] < cp

    @pl.when(jnp.logical_and(is_last, in_prompt))
    def _():
        finish(xp_ref, op_ref)

    @pl.when(jnp.logical_and(is_last, jnp.logical_not(in_prompt)))
    def _():
        finish(xs_ref, os_ref)


def _combine(x_p, x_s, ys, slot0_rep, slot1_rep, g_final, sched, *, tm, tc, final_norm):
    d = x_p.shape[1]
    cp = x_p.shape[0] // tc
    chunk, tile, first, last, valid = sched
    tok = pl.BlockSpec((tc, LANES), lambda s, c, t, f, l, v: (c[s], 0))
    xp_spec, xs_spec = _group_specs(tc, d, cp, lambda pre: pre[0])
    grid_spec = pltpu.PrefetchScalarGridSpec(
        num_scalar_prefetch=5,
        grid=(chunk.shape[0],),
        in_specs=[tok, tok,
                  pl.BlockSpec((tm, d), lambda s, c, t, f, l, v: (t[s], 0)),
                  xp_spec, xs_spec,
                  pl.BlockSpec((1, d), lambda s, c, t, f, l, v: (0, 0))],
        out_specs=[xp_spec, xs_spec],
        scratch_shapes=[pltpu.VMEM((tc, d), F32)],
    )
    return pl.pallas_call(
        functools.partial(_combine_body, tm=tm, cp=cp, final_norm=final_norm),
        grid_spec=grid_spec,
        out_shape=[jax.ShapeDtypeStruct(x_p.shape, F32), jax.ShapeDtypeStruct(x_s.shape, F32)],
        compiler_params=_cparams(1),
        name="combine",
    )(chunk, tile, first, last, valid, slot0_rep, slot1_rep, ys, x_p, x_s, g_final.reshape(1, d))


def _flatten_ranges(lengths, n_steps):
    end = jnp.cumsum(lengths)
    total = end[-1]
    s = jnp.arange(n_steps, dtype=I32)
    sc = jnp.minimum(s, total - 1)
    seg = jnp.sum((end[None, :] <= sc[:, None]).astype(I32), axis=1)
    off = sc - (end - lengths)[seg]
    return seg, off, (s < total).astype(I32)


def _route(eidx, gates, *, tm, tc):
    n = eidx.shape[0]
    n_pairs = n * TOP_K
    n_chunks = n // tc
    n_tiles = n_pairs // tm + N_EXPERTS
    n_steps = n_tiles + N_EXPERTS * n_chunks
    experts = jnp.arange(N_EXPERTS, dtype=I32)

    flat_e = eidx.reshape(-1)
    onehot = (flat_e[:, None] == experts[None, :]).astype(I32)
    csum = jnp.cumsum(onehot, axis=0)
    rank = jnp.sum(csum * onehot, axis=1) - 1
    counts = csum[-1]
    tiles_per = (counts + tm - 1) // tm
    tile_end = jnp.cumsum(tiles_per)
    tile_start = tile_end - tiles_per
    offs = tile_start * tm
    slot = jnp.sum(onehot * offs[None, :], axis=1) + rank
    n_valid = tile_end[-1:]

    upd = jnp.stack([(jnp.arange(n_pairs, dtype=I32) // TOP_K).astype(F32), gates.reshape(-1)], axis=1)
    init = jnp.concatenate([jnp.full((n_tiles * tm, 1), -1.0, F32), jnp.zeros((n_tiles * tm, 1), F32)], axis=1)
    per_slot = init.at[slot].set(upd)
    src_rep = jnp.broadcast_to(per_slot[:, 0:1].astype(I32), (n_tiles * tm, LANES))
    gate_rep = jnp.broadcast_to(per_slot[:, 1:2], (n_tiles * tm, LANES))

    t = jnp.arange(n_tiles, dtype=I32)
    used = t < n_valid[0]
    te = jnp.minimum(jnp.sum((tile_end[None, :] <= t[:, None]).astype(I32), axis=1), N_EXPERTS - 1)
    tile_expert = jnp.where(used, te, te[n_valid[0] - 1])
    r_first = (t - tile_start[te]) * tm
    r_last = jnp.minimum(r_first + tm, counts[te]) - 1
    cs_t = csum[:, te]
    p_first = jnp.sum((cs_t <= r_first[None, :]).astype(I32), axis=0)
    p_last = jnp.sum((cs_t <= r_last[None, :]).astype(I32), axis=0)
    c_lo = (p_first // TOP_K) // tc
    c_hi = (p_last // TOP_K) // tc
    d_tile, d_off, d_valid = _flatten_ranges(jnp.where(used, c_hi - c_lo + 1, 1), n_steps)
    d_chunk = jnp.where(used[d_tile], c_lo[d_tile] + d_off, 0)
    d_first = d_valid * (d_off == 0).astype(I32)
    dispatch_sched = (d_tile, d_chunk, d_first, d_valid)

    before = jnp.concatenate([jnp.zeros((1, N_EXPERTS), I32), csum[TOP_K * tc - 1::TOP_K * tc]], axis=0)
    cnt = before[1:] - before[:-1]
    lo_slot = offs[None, :] + before[:-1]
    lo_tile = (lo_slot // tm).reshape(-1)
    hi_tile = ((lo_slot + cnt - 1) // tm).reshape(-1)
    span = jnp.where(cnt.reshape(-1) > 0, hi_tile - lo_tile + 1, 0)
    c_seg, c_off, c_valid = _flatten_ranges(span, n_steps)
    c_chunk = c_seg // N_EXPERTS
    c_tile = lo_tile[c_seg] + c_off
    prev_chunk = jnp.concatenate([jnp.full((1,), -1, I32), c_chunk[:-1]])
    next_chunk = jnp.concatenate([c_chunk[1:], jnp.full((1,), -1, I32)])
    next_valid = jnp.concatenate([c_valid[1:], jnp.zeros((1,), I32)])
    c_first = c_valid * (c_chunk != prev_chunk).astype(I32)
    c_last = c_valid * jnp.logical_or(c_chunk != next_chunk, next_valid == 0).astype(I32)
    combine_sched = (c_chunk, c_tile, c_first, c_last, c_valid)

    slot2 = slot.reshape(n, TOP_K)
    slot0_rep = jnp.broadcast_to(slot2[:, 0:1], (n, LANES))
    slot1_rep = jnp.broadcast_to(slot2[:, 1:2], (n, LANES))
    return src_rep, gate_rep, tile_expert, n_valid, dispatch_sched, slot0_rep, slot1_rep, combine_sched


def kernel(x_prompt, x_sample, cache_k, cache_v, page_table, norm_mix, w_in, lam_q1, lam_k1, lam_q2, lam_k2,
           attn_head_gain, sgu_ln_g, sgu_ln_b, sgu_w, sgu_b, w_out, norm_ffn, ffn_w_gate, ffn_w_up,
           ffn_w_down, moe_router, moe_w_gate, moe_w_up, moe_w_down, norm_final):
    depth = w_in.shape[0]
    b, t, d = x_prompt.shape
    n_s, s_len, _ = x_sample.shape
    n_p = b * t
    n_sr = n_s * s_len
    n_pool = cache_k.shape[1]
    assert depth % 2 == 0, "the last layer is expected to be a routed layer"
    assert CHUNK % s_len == 0

    n_pg = t // PAGE_SIZE
    cache_kt = jnp.swapaxes(cache_k.reshape(depth, n_pool, PAGE_SIZE, d), 2, 3)
    cache_vr = cache_v.reshape(depth, n_pool, PAGE_SIZE * N_HEADS, V_DIM)
    xp = x_prompt.reshape(n_p, d)
    xs = x_sample.reshape(n_sr, d)
    tm_p, tm_s = 512, 512

    k_p, v_p, k_s, v_s, sv_s = [], [], [], [], []
    for l in range(depth):
        lam_init = 0.8 - 0.6 * math.exp(-0.3 * l)
        lamp = jnp.stack([lam_q1[l], lam_k1[l], lam_q2[l], lam_k2[l]])
        w_in_bf = w_in[l].astype(BF16)
        w_out_bf = w_out[l].astype(BF16)
        gain = attn_head_gain[l]
        ws_p = sgu_w[l]
        bt_p = sgu_b[l].T
        reps = CHUNK // s_len
        eye = jnp.eye(reps, dtype=F32)
        ws_s = jax.vmap(lambda w: jnp.kron(eye, w))(sgu_w[l][:, :s_len, :s_len])
        bt_s = jnp.tile(sgu_b[l][:, :s_len].T, (reps, 1))
        routed = l % 2 == 1
        j = l // 2
        router_pad = None
        if routed:
            r_t = moe_router[j].T
            r_hi = r_t.astype(BF16)
            r_lo = (r_t - r_hi.astype(F32)).astype(BF16)
            router_pad = jnp.stack([jnp.concatenate([r_hi, r_lo]), jnp.concatenate([r_hi, jnp.zeros_like(r_hi)])])

        q, k, v, u, vn, sga, sgs, v_bf = _proj(xp, norm_mix[l], w_in_bf, sgu_ln_g[l], sgu_ln_b[l],
                                               tm=tm_p, paged=True)
        q2, k2, v2, u2, vn2, sga2, sgs2, vnf = _proj(xs, norm_mix[l], w_in_bf, sgu_ln_g[l], sgu_ln_b[l],
                                                     tm=tm_s, paged=False)
        ag, ag2 = _attn(q.reshape(b, t, d), k.reshape(b, n_pg, d, PAGE_SIZE), v_bf.reshape(b, t, d),
                        sga.reshape(b, t, d), q2.reshape(n_s, s_len, d), k2.reshape(n_s, s_len, d),
                        v2.reshape(n_s, s_len, d), sga2.reshape(n_s, s_len, d), cache_kt, cache_vr, page_table,
                        gain, lamp, layer=l, lam_init=lam_init, tq=512)
        mix_p = _mix_out(ag.reshape(n_p, d), u, vn, sgs, xp, ws_p, bt_p, w_out_bf, norm_ffn[l], router_pad, tm=tm_p)
        mix_s = _mix_out(ag2.reshape(n_sr, d), u2, vn2, sgs2, xs, ws_s, bt_s, w_out_bf, norm_ffn[l], router_pad,
                         tm=tm_s)
        k_p.append(k)
        v_p.append(v)
        k_s.append(k2)
        v_s.append(v2)
        sv_s.append(vnf)

        if not routed:
            wg, wu, wd = (ffn_w_gate[j:j + 1].astype(BF16), ffn_w_up[j:j + 1].astype(BF16),
                          ffn_w_down[j:j + 1].astype(BF16))
            tf = wg.shape[2] // 2
            outs = []
            for (x_new, h), tm in ((mix_p, tm_p), (mix_s, tm_s)):
                nt = x_new.shape[0] // tm
                outs.append(_ffn(h, wg, wu, wd, jnp.zeros((nt,), I32), jnp.full((1,), nt, I32),
                                 x_new, tm=tm, tf=tf, routed=False))
            xp, xs = outs
        else:
            tm = tc = 512
            assert n_p % tc == 0 and n_sr % tc == 0
            eidx =jnp.concatenate([mix_p[2][:TOP_K], mix_s[2][:TOP_K]], axis=1).T
            gates = jnp.concatenate([mix_p[3][:TOP_K], mix_s[3][:TOP_K]], axis=1).T
            (src_rep, gate_rep, tile_expert, n_valid, dispatch_sched,
             slot0_rep, slot1_rep, combine_sched) = _route(eidx, gates, tm=tm, tc=tc)
            hs = _dispatch(mix_p[1], mix_s[1], src_rep, dispatch_sched, tm=tm, tc=tc)
            ys = _ffn(hs, moe_w_gate[j].astype(BF16), moe_w_up[j].astype(BF16), moe_w_down[j].astype(BF16),
                      tile_expert, n_valid, gate_rep, tm=tm, tf=moe_w_gate.shape[3] // 2, routed=True)
            xp, xs = _combine(mix_p[0], mix_s[0], ys, slot0_rep, slot1_rep, norm_final, combine_sched,
                              tm=tm, tc=tc, final_norm=(l == depth - 1))

    y_prompt = xp.reshape(b, t, d)
    y_sample = xs.reshape(n_s, s_len, d)
    k_prompt = jnp.moveaxis(jnp.stack(k_p).reshape(depth, b, n_pg, N_HEADS, 2, HEAD_DIM, PAGE_SIZE), 6, 3)
    v_prompt = jnp.stack(v_p).reshape(depth, b, n_pg, PAGE_SIZE, N_HEADS, V_DIM)
    k_sample = jnp.stack(k_s).reshape(depth, n_s, s_len, N_HEADS, 2, HEAD_DIM)
    v_sample = jnp.stack(v_s).reshape(depth, n_s, s_len, N_HEADS, V_DIM)
    sgu_v_sample = jnp.stack(sv_s).reshape(depth, n_s, s_len, d)
    return (y_prompt, y_sample, k_prompt, v_prompt, k_sample, v_sample, sgu_v_sample)
```

```python
import functools
import math

import jax
import jax.numpy as jnp
from jax import lax
from jax.experimental import pallas as pl
from jax.experimental.pallas import tpu as pltpu

F32 = jnp.float32
BF16 = jnp.bfloat16
I32 = jnp.int32

N_HEADS = 8
HEAD_DIM = 64
V_DIM = 2 * HEAD_DIM
PAIR = 2 * V_DIM
CHUNK = 128
N_GROUPS = 8
GROUP_DIM = 128
N_EXPERTS = 8
TOP_K = 2
PAGE_SIZE = 128
EPS = 1e-6
NEG = -1e30
Q_PRESCALE = HEAD_DIM ** -0.5 * math.log2(math.e)
N_SEG = 7
LANES = 128
VMEM_LIMIT = 56 * 1024 * 1024

NT_DIMS = (((1,), (1,)), ((), ()))


def _cparams(n_axes, vmem=VMEM_LIMIT):
    return pltpu.CompilerParams(dimension_semantics=("arbitrary",) * n_axes,
                                vmem_limit_bytes=vmem)


def _rms(x, gain):
    return x * lax.rsqrt(jnp.mean(x * x, axis=-1, keepdims=True) + EPS) * gain


def _lam_value(lamp_ref, lam_init):
    a = jnp.sum(lamp_ref[0:1, :] * lamp_ref[1:2, :], axis=-1, keepdims=True)
    b = jnp.sum(lamp_ref[2:3, :] * lamp_ref[3:4, :], axis=-1, keepdims=True)
    return jnp.exp(a) - jnp.exp(b) + lam_init


def _one_hot_cols(rel, width):
    lane = lax.broadcasted_iota(I32, rel.shape, 1)
    return jnp.concatenate([jnp.where(rel == lane + q * LANES, 1.0, 0.0) for q in range(width // LANES)], axis=1)


def _proj_body(x_ref, g_ref, w_ref, lng_ref, lnb_ref, *rest, paged, first_layer):
    q_ref, k_ref, v_ref, u_ref, vn_ref, sga_ref, sgs_ref, extra_ref = rest[-8:]
    tm, d = x_ref.shape
    h = _rms(x_ref[...], g_ref[...]).astype(BF16)

    def seg(s):
        return jnp.dot(h, w_ref[:, s * d:(s + 1) * d], preferred_element_type=F32)

    q_ref[...] = (seg(0) * Q_PRESCALE).astype(BF16)
    if paged:
        if first_layer:
            if k_ref.shape[0] > 1:
                k_ref[1:] = jnp.zeros((k_ref.shape[0] - 1,) + k_ref.shape[1:], F32)
                v_ref[1:] = jnp.zeros((v_ref.shape[0] - 1,) + v_ref.shape[1:], F32)
            k_ref, v_ref = k_ref.at[0], v_ref.at[0]
        kk = seg(1)
        for pg in range(tm // PAGE_SIZE):
            k_ref[pg] = kk[pg * PAGE_SIZE:(pg + 1) * PAGE_SIZE, :].T
        vv = seg(2)
        for hd in range(N_HEADS):
            v_ref[:, hd, :] = vv[:, hd * V_DIM:(hd + 1) * V_DIM]
        extra_ref[...] = vv.astype(BF16)
    else:
        k_ref[...] = seg(1)
        v_ref[...] = seg(2)
    u_ref[...] = jax.nn.gelu(seg(3)).astype(BF16)
    vg = jax.nn.gelu(seg(4))
    xc = vg - jnp.mean(vg, axis=-1, keepdims=True)
    vn = xc * lax.rsqrt(jnp.mean(xc * xc, axis=-1, keepdims=True) + EPS) * lng_ref[...] + lnb_ref[...]
    vn_ref[...] = vn.astype(BF16)
    if not paged:
        extra_ref[...] = vn
    sga_ref[...] = jax.nn.sigmoid(seg(5)).astype(BF16)
    sgs_ref[...] = jax.nn.sigmoid(seg(6)).astype(BF16)


def _proj(x, gain, w_bf, ln_g, ln_b, *, tm, paged, layer=0, depth=1, stacks=None):
    n, d = x.shape
    row = pl.BlockSpec((tm, d), lambda i: (i, 0))
    vec = pl.BlockSpec((1, d), lambda i: (0, 0))
    rows_bf = jax.ShapeDtypeStruct((n, d), BF16)
    rows_f32 = jax.ShapeDtypeStruct((n, d), F32)
    in_specs = [row, vec, pl.BlockSpec((d, N_SEG * d), lambda i: (0, 0), pipeline_mode=pl.Buffered(1)), vec, vec]
    args = [x, gain.reshape(1, d), w_bf, ln_g.reshape(1, d), ln_b.reshape(1, d)]
    aliases = {}
    if paged:
        first = stacks is None
        assert first == (layer == 0)
        slot = depth if first else None
        k_spec = pl.BlockSpec((slot, tm // PAGE_SIZE, d, PAGE_SIZE), lambda i: (0 if first else layer, i, 0, 0))
        k_shape = jax.ShapeDtypeStruct((depth, n // PAGE_SIZE, d, PAGE_SIZE), F32)
        v_spec = pl.BlockSpec((slot, tm, N_HEADS, V_DIM), lambda i: (0 if first else layer, i, 0, 0))
        v_shape = jax.ShapeDtypeStruct((depth, n, N_HEADS, V_DIM), F32)
        if not first:
            in_specs += [pl.BlockSpec(memory_space=pl.ANY)] * 2
            aliases = {len(args): 1, len(args) + 1: 2}
            args += list(stacks)
    else:
        k_spec, k_shape, v_spec, v_shape = row, rows_f32, row, rows_f32
    out_specs = [row, k_spec, v_spec, row, row, row, row, row]
    out_shape = [rows_bf, k_shape, v_shape, rows_bf, rows_bf, rows_bf, rows_bf, rows_bf if paged else rows_f32]
    return pl.pallas_call(
        functools.partial(_proj_body, paged=paged, first_layer=paged and stacks is None),
        grid=(n // tm,),
        in_specs=in_specs,
        out_specs=out_specs,
        out_shape=out_shape,
        input_output_aliases=aliases,
        compiler_params=_cparams(1),
        name="proj",
    )(*args)


def _prompt_attn_tile(i, q, sga, gain_ref, lamp_ref, kb, vb, qs, acc, m_s, l_s, *, tq, lam_init):
    lane = lax.broadcasted_iota(I32, q.shape, 1)
    zero = jnp.zeros_like(q)
    qs[0:tq, :] = jnp.where(lane < HEAD_DIM, q, zero)
    qs[tq:, :] = jnp.where(lane >= HEAD_DIM, q, zero)
    m_s[...] = jnp.full(m_s.shape, NEG, F32)
    l_s[...] = jnp.zeros(l_s.shape, F32)
    acc[...] = jnp.zeros(acc.shape, F32)

    def step(start, width, masked):
        s = jnp.dot(qs[...], kb[:, pl.ds(start, width)], preferred_element_type=F32)
        if masked:
            r = lax.broadcasted_iota(I32, s.shape, 0)
            r = jnp.where(r >= tq, r - tq, r)
            c = lax.broadcasted_iota(I32, s.shape, 1)
            s = jnp.where(c <= r, s, NEG)
        m_prev = m_s[...]
        m_new = jnp.maximum(m_prev, jnp.max(s, axis=-1, keepdims=True))
        alpha = jnp.exp2(m_prev - m_new)
        p = jnp.exp2(s - jnp.tile(m_new, (1, width // LANES)))
        l_s[...] = alpha * l_s[...] + jnp.sum(p, axis=-1, keepdims=True)
        acc[...] = alpha * acc[...] + jnp.dot(p.astype(BF16), vb[pl.ds(start, width), :],
                                              preferred_element_type=F32)
        m_s[...] = m_new

    def off_diag_pair(jj, carry):
        step(pl.multiple_of(jj * 2 * tq, 2 * tq), 2 * tq, False)
        return carry

    lax.fori_loop(0, i // 2, off_diag_pair, 0)

    @pl.when(i % 2 == 1)
    def _():
        step(pl.multiple_of((i - 1) * tq, tq), tq, False)

    step(pl.multiple_of(i * tq, tq), tq, True)

    o = acc[...] / l_s[...]
    lam = _lam_value(lamp_ref, lam_init)
    od = o[0:tq, :] - lam * o[tq:, :]
    y = _rms(od, gain_ref[...]) * (1.0 - lam_init)
    return (sga.astype(F32) * y).astype(BF16)


def _sample_attn_seq(slot, q_ref, kn_ref, vn_ref, sga_ref, gain_ref, lamp_ref, kbuf, vbuf, o_ref,
                     *, n_pages, s_len, lam_init):
    n_pairs = N_HEADS // 2
    prow = 4 * s_len
    r = lax.broadcasted_iota(I32, (prow, PAIR), 0)
    c = lax.broadcasted_iota(I32, (prow, PAIR), 1)
    on_diag = (c // HEAD_DIM) == (r // s_len)
    qp = []
    for hp in range(n_pairs):
        qt = jnp.tile(q_ref[:, hp * PAIR:(hp + 1) * PAIR], (4, 1))
        qp.append(jnp.where(on_diag, qt, jnp.zeros_like(qt)))

    def pair_rows(fn):
        return jnp.concatenate([fn(hp) for hp in range(n_pairs)], axis=0)

    s_past = [pair_rows(lambda hp: jnp.dot(qp[hp], kbuf[slot, p, hp * PAIR:(hp + 1) * PAIR, :].astype(BF16),
                                           preferred_element_type=F32))
              for p in range(n_pages)]
    s_new = pair_rows(lambda hp: lax.dot_general(qp[hp], kn_ref[:, hp * PAIR:(hp + 1) * PAIR].astype(BF16),
                                                 NT_DIMS, preferred_element_type=F32))
    rn = lax.broadcasted_iota(I32, s_new.shape, 0) % s_len
    cn = lax.broadcasted_iota(I32, s_new.shape, 1)
    s_new = jnp.where(cn <= rn, s_new, NEG)

    m = jnp.max(s_new, axis=-1, keepdims=True)
    for sp in s_past:
        m = jnp.maximum(m, jnp.max(sp, axis=-1, keepdims=True))
    p_new = jnp.exp2(s_new - m)
    l = jnp.sum(p_new, axis=-1, keepdims=True)
    p_new = p_new.astype(BF16)
    o = [jnp.dot(p_new[hp * prow:(hp + 1) * prow], vn_ref[:, hp * PAIR:(hp + 1) * PAIR].astype(BF16),
                 preferred_element_type=F32) for hp in range(n_pairs)]
    for p in range(n_pages):
        pp = jnp.exp2(s_past[p] - m)
        l = l + jnp.sum(pp, axis=-1, keepdims=True)
        pp = pp.astype(BF16)
        for hp in range(n_pairs):
            v_pair = jnp.concatenate([vbuf[slot, p, pl.ds(2 * hp + hl, PAGE_SIZE, stride=N_HEADS), :]
                                      for hl in range(2)], axis=1)
            o[hp] = o[hp] + jnp.dot(pp[hp * prow:(hp + 1) * prow], v_pair.astype(BF16),
                                    preferred_element_type=F32)

    lam = _lam_value(lamp_ref, lam_init)
    for h in range(N_HEADS):
        hp, hl = divmod(h, 2)
        cols = slice(h * V_DIM, (h + 1) * V_DIM)
        r0 = hl * 2 * s_len
        oh = o[hp][r0:r0 + 2 * s_len, hl * V_DIM:(hl + 1) * V_DIM] / l[h * 2 * s_len:(h + 1) * 2 * s_len]
        od = oh[0:s_len] - lam * oh[s_len:2 * s_len]
        y = _rms(od, gain_ref[h:h + 1, :]) * (1.0 - lam_init)
        o_ref[:, cols] = (sga_ref[:, cols].astype(F32) * y).astype(BF16)


def _attn_body(pt_ref, qp_ref, kt_ref, vb_ref, sgap_ref, gainp_ref, lamp_ref,
               qs_ref, kn_ref, vn_ref, sgas_ref, gains_ref, ck_ref, cv_ref, op_ref, os_ref,
               kb, qst, acc, m_s, l_s, kbuf, vbuf, sem,
               *, layer, n_pages, s_len, tq, tiles, steps_per_head, lam_init):
    n = pl.program_id(0)
    n_seq = pl.num_programs(0)

    def page_copies(seq, slot):
        out = []
        for p in range(n_pages):
            page = pt_ref[seq * n_pages + p]
            out.append(pltpu.make_async_copy(ck_ref.at[layer, page], kbuf.at[slot, p], sem.at[0, slot]))
            out.append(pltpu.make_async_copy(cv_ref.at[layer, page], vbuf.at[slot, p], sem.at[1, slot]))
        return out

    @pl.when(n == 0)
    def _():
        for cp in page_copies(0, 0):
            cp.start()

    @pl.when(n + 1 < n_seq)
    def _():
        for cp in page_copies(n + 1, (n + 1) % 2):
            cp.start()

    part = n % steps_per_head

    @pl.when(part == 0)
    def _():
        for pg in range(kt_ref.shape[0]):
            kb[:, pg * PAGE_SIZE:(pg + 1) * PAGE_SIZE] = kt_ref[pg].astype(BF16)

    for t in range(tiles):
        rows = slice(t * tq, (t + 1) * tq)
        op_ref[rows, :] = _prompt_attn_tile(part * tiles + t, qp_ref[rows, :], sgap_ref[rows, :], gainp_ref,
                                            lamp_ref, kb, vb_ref, qst, acc, m_s, l_s, tq=tq, lam_init=lam_init)

    slot = n % 2
    for cp in page_copies(n, slot):
        cp.wait()
    _sample_attn_seq(slot, qs_ref, kn_ref, vn_ref, sgas_ref, gains_ref, lamp_ref, kbuf, vbuf, os_ref,
                     n_pages=n_pages, s_len=s_len, lam_init=lam_init)


def _attn(q_p, kt_p, v_p, sga_p, q_s, k_new, v_new, sga_s, cache_kt, cache_v, page_table, gain, lamp,
          *, layer, lam_init, tq):
    b, t, d = q_p.shape
    n_seq, s_len, _ = q_s.shape
    n_pages = page_table.shape[1]
    assert V_DIM == LANES
    nq = t // tq
    tiles = (b * N_HEADS * nq) // n_seq
    assert tiles * n_seq == b * N_HEADS * nq and nq % tiles == 0
    steps_per_head = nq // tiles

    def bh(n):
        return n // (steps_per_head * N_HEADS), (n // steps_per_head) % N_HEADS

    qspec = pl.BlockSpec((None, tiles * tq, V_DIM), lambda n, pt: (bh(n)[0], n % steps_per_head, bh(n)[1]))
    kspec = pl.BlockSpec((None, None, t // PAGE_SIZE, V_DIM, PAGE_SIZE),
                         lambda n, pt: (layer, bh(n)[0], 0, bh(n)[1], 0))
    vspec = pl.BlockSpec((None, t, V_DIM), lambda n, pt: (bh(n)[0], 0, bh(n)[1]))
    row = pl.BlockSpec((None, s_len, d), lambda n, pt: (n, 0, 0))
    grid_spec = pltpu.PrefetchScalarGridSpec(
        num_scalar_prefetch=1,
        grid=(n_seq,),
        in_specs=[qspec, kspec, vspec, qspec,
                  pl.BlockSpec((None, 1, V_DIM), lambda n, pt: (bh(n)[1], 0, 0)),
                  pl.BlockSpec(lamp.shape, lambda n, pt: (0, 0)),
                  row, row, row, row,
                  pl.BlockSpec(gain.shape, lambda n, pt: (0, 0)),
                  pl.BlockSpec(memory_space=pl.ANY), pl.BlockSpec(memory_space=pl.ANY)],
        out_specs=[qspec, row],
        scratch_shapes=[pltpu.VMEM((V_DIM, t), BF16),
                        pltpu.VMEM((2 * tq, V_DIM), BF16), pltpu.VMEM((2 * tq, V_DIM), F32),
                        pltpu.VMEM((2 * tq, LANES), F32), pltpu.VMEM((2 * tq, LANES), F32),
                        pltpu.VMEM((2, n_pages, d, PAGE_SIZE), F32),
                        pltpu.VMEM((2, n_pages, PAGE_SIZE * N_HEADS, V_DIM), F32),
                        pltpu.SemaphoreType.DMA((2, 2))],
    )
    return pl.pallas_call(
        functools.partial(_attn_body, layer=layer, n_pages=n_pages, s_len=s_len, tq=tq, tiles=tiles,
                          steps_per_head=steps_per_head, lam_init=lam_init),
        grid_spec=grid_spec,
        out_shape=[jax.ShapeDtypeStruct((b, t, d), BF16), jax.ShapeDtypeStruct((n_seq, s_len, d), BF16)],
        compiler_params=_cparams(1),
        name="attn",
    )(page_table.reshape(-1), q_p, kt_p, v_p, sga_p, gain.reshape(N_HEADS, 1, V_DIM), lamp,
      q_s, k_new, v_new, sga_s, gain, cache_kt, cache_v)


def _mix_body(ag_ref, u_ref, vn_ref, sgs_ref, x_ref, ws_ref, bt_ref, wout_ref, gffn_ref, *rest,
              with_router):
    if with_router:
        router_ref, xo_ref, h_ref, eidx_ref, gate_ref, mixed = rest
    else:
        xo_ref, h_ref, mixed = rest
    tm = x_ref.shape[0]
    rr = lax.broadcasted_iota(I32, (CHUNK, CHUNK), 0)
    cc = lax.broadcasted_iota(I32, (CHUNK, CHUNK), 1)
    tril = cc <= rr
    for g in range(N_GROUPS):
        wg = jnp.where(tril, ws_ref[g], 0.0).astype(BF16)
        bias = bt_ref[:, g:g + 1]
        cols = slice(g * GROUP_DIM, (g + 1) * GROUP_DIM)
        for ci in range(tm // CHUNK):
            rows = slice(ci * CHUNK, (ci + 1) * CHUNK)
            s = jnp.dot(wg, vn_ref[rows, cols], preferred_element_type=F32) + bias
            sgu = u_ref[rows, cols].astype(F32) * s
            mix = ag_ref[rows, cols].astype(F32) + sgs_ref[rows, cols].astype(F32) * sgu
            mixed[rows, cols] = mix.astype(BF16)
    x_new = x_ref[...] + jnp.dot(mixed[...], wout_ref[...], preferred_element_type=F32)
    xo_ref[...] = x_new
    h = _rms(x_new, gffn_ref[...])
    h_ref[...] = h.astype(BF16)
    if with_router:
        h_hi = h.astype(BF16)
        h_lo = (h - h_hi.astype(F32)).astype(BF16)
        a = lax.dot_general(router_ref[0], h_hi, NT_DIMS, preferred_element_type=F32)
        b = lax.dot_general(router_ref[1], h_lo, NT_DIMS, preferred_element_type=F32)
        lg = a[0:N_EXPERTS] + a[N_EXPERTS:] + b[0:N_EXPERTS]
        ex = lax.broadcasted_iota(I32, lg.shape, 0)
        m1 = jnp.max(lg, axis=0, keepdims=True)
        i1 = jnp.min(jnp.where(lg == m1, ex, N_EXPERTS), axis=0, keepdims=True)
        lg2 = jnp.where(ex == i1, -jnp.inf, lg)
        m2 = jnp.max(lg2, axis=0, keepdims=True)
        i2 = jnp.min(jnp.where(lg2 == m2, ex, N_EXPERTS), axis=0, keepdims=True)
        e = jnp.exp(m2 - m1)
        g1 = 1.0 / (1.0 + e)
        g2 = e / (1.0 + e)
        eidx_ref[...] = jnp.where(ex == 0, i1, jnp.where(ex == 1, i2, 0))
        gate_ref[...] = jnp.where(ex == 0, g1, jnp.where(ex == 1, g2, 0.0))


def _mix_out(ag, u, vn, sgs, x, ws, bt, wout_bf, g_ffn, router_pad, *, tm):
    n, d = x.shape
    with_router = router_pad is not None
    row = pl.BlockSpec((tm, d), lambda i: (i, 0))
    per_expert = pl.BlockSpec((N_EXPERTS, tm), lambda i: (0, i))

    def full(a):
        return pl.BlockSpec(a.shape, lambda i: (0,) * a.ndim)

    g2 = g_ffn.reshape(1, d)
    in_specs = [row, row, row, row, row, full(ws), full(bt), full(wout_bf), full(g2)]
    args = [ag, u, vn, sgs, x, ws, bt, wout_bf, g2]
    out_specs = [row, row]
    out_shape = [jax.ShapeDtypeStruct((n, d), F32), jax.ShapeDtypeStruct((n, d), BF16)]
    if with_router:
        in_specs.append(full(router_pad))
        args.append(router_pad)
        out_specs += [per_expert, per_expert]
        out_shape += [jax.ShapeDtypeStruct((N_EXPERTS, n), I32), jax.ShapeDtypeStruct((N_EXPERTS, n), F32)]
    return pl.pallas_call(
        functools.partial(_mix_body, with_router=with_router),
        grid=(n // tm,),
        in_specs=in_specs,
        out_specs=out_specs,
        out_shape=out_shape,
        scratch_shapes=[pltpu.VMEM((tm, d), BF16)],
        compiler_params=_cparams(1),
        name="mix_out",
    )(*args)


def _ffn_body(te_ref, nv_ref, x_ref, wg_ref, wu_ref, wd_ref, side_ref, o_ref, acc, *, routed):
    del te_ref
    i = pl.program_id(0)
    j = pl.program_id(1)
    last = pl.num_programs(1) - 1
    valid = i < nv_ref[0]

    @pl.when(valid)
    def _():
        h = x_ref[...]
        a = jax.nn.silu(jnp.dot(h, wg_ref[...], preferred_element_type=F32))
        a = (a * jnp.dot(h, wu_ref[...], preferred_element_type=F32)).astype(BF16)
        y = jnp.dot(a, wd_ref[...], preferred_element_type=F32)

        @pl.when(j == 0)
        def _():
            acc[...] = y

        @pl.when(j > 0)
        def _():
            acc[...] += y

        @pl.when(j == last)
        def _():
            if routed:
                o_ref[...] = (acc[...] * jnp.tile(side_ref[...], (1, acc.shape[1] // LANES))).astype(o_ref.dtype)
            else:
                o_ref[...] = side_ref[...] + acc[...]

    @pl.when(jnp.logical_and(jnp.logical_not(valid), j == last))
    def _():
        o_ref[...] = jnp.zeros(o_ref.shape, o_ref.dtype)


def _ffn(x, wg, wu, wd, tile_expert, n_valid, side, *, tm, tf, routed):
    n, d = x.shape
    f = wg.shape[2]

    def row_idx(i, j, te, nv):
        return (jnp.minimum(i, nv[0] - 1), 0)

    side_spec = pl.BlockSpec((tm, side.shape[1]), row_idx)
    grid_spec = pltpu.PrefetchScalarGridSpec(
        num_scalar_prefetch=2,
        grid=(n // tm, f // tf),
        in_specs=[pl.BlockSpec((tm, d), row_idx),
                  pl.BlockSpec((None, d, tf), lambda i, j, te, nv: (te[i], 0, j)),
                  pl.BlockSpec((None, d, tf), lambda i, j, te, nv: (te[i], 0, j)),
                  pl.BlockSpec((None, tf, d), lambda i, j, te, nv: (te[i], j, 0)),
                  side_spec],
        out_specs=pl.BlockSpec((tm, d), lambda i, j, te, nv: (i, 0)),
        scratch_shapes=[pltpu.VMEM((tm, d), F32)],
    )
    return pl.pallas_call(
        functools.partial(_ffn_body, routed=routed),
        grid_spec=grid_spec,
        out_shape=jax.ShapeDtypeStruct((n, d), BF16 if routed else F32),
        compiler_params=_cparams(2),
        name="ffn",
    )(tile_expert, n_valid, x, wg, wu, wd, side)


def _dispatch_body(tile_ref, chunk_ref, first_ref, valid_ref, src_ref, hp_ref, hs_ref, o_ref, *, tc, cp):
    del tile_ref
    s = pl.program_id(0)

    @pl.when(valid_ref[s] == 1)
    def _():
        chunk = chunk_ref[s]
        rel = src_ref[...] - chunk * tc
        pick = _one_hot_cols(rel, tc).astype(BF16)
        h = jnp.where(chunk < cp, hp_ref[...], hs_ref[...])
        rows = jnp.dot(pick, h, preferred_element_type=F32).astype(BF16)

        @pl.when(first_ref[s] == 1)
        def _():
            o_ref[...] = rows

        @pl.when(first_ref[s] == 0)
        def _():
            o_ref[...] += rows


def _group_specs(tc, width, cp, chunk_of):
    return (pl.BlockSpec((tc, width), lambda s, *pre: (jnp.minimum(chunk_of(pre)[s], cp - 1), 0)),
            pl.BlockSpec((tc, width), lambda s, *pre: (jnp.maximum(chunk_of(pre)[s] - cp, 0), 0)))


def _dispatch(h_p, h_s, src_rep, sched, *, tm, tc):
    d = h_p.shape[1]
    cp = h_p.shape[0] // tc
    n_slots = src_rep.shape[0]
    tile, chunk, first, valid = sched
    hp_spec, hs_spec = _group_specs(tc, d, cp, lambda pre: pre[1])
    grid_spec = pltpu.PrefetchScalarGridSpec(
        num_scalar_prefetch=4,
        grid=(tile.shape[0],),
        in_specs=[pl.BlockSpec((tm, LANES), lambda s, t, c, f, v: (t[s], 0)), hp_spec, hs_spec],
        out_specs=pl.BlockSpec((tm, d), lambda s, t, c, f, v: (t[s], 0)),
    )
    return pl.pallas_call(
        functools.partial(_dispatch_body, tc=tc, cp=cp),
        grid_spec=grid_spec,
        out_shape=jax.ShapeDtypeStruct((n_slots, d), BF16),
        compiler_params=_cparams(1),
        name="dispatch",
    )(tile, chunk, first, valid, src_rep, h_p, h_s)


def _combine_body(chunk_ref, tile_ref, first_ref, last_ref, valid_ref,
                  s0_ref, s1_ref, ys_ref, xp_ref, xs_ref, gfin_ref, op_ref, os_ref, acc, *, tm, cp, final_norm):
    s = pl.program_id(0)

    @pl.when(first_ref[s] == 1)
    def _():
        acc[...] = jnp.zeros(acc.shape, F32)

    @pl.when(valid_ref[s] == 1)
    def _():
        base = tile_ref[s] * tm
        hit = _one_hot_cols(s0_ref[...] - base, tm) + _one_hot_cols(s1_ref[...] - base, tm)
        acc[...] += jnp.dot(hit.astype(BF16), ys_ref[...], preferred_element_type=F32)

    def finish(x_ref, o_ref):
        x = x_ref[...] + acc[...]
        o_ref[...] = _rms(x, gfin_ref[...]) if final_norm else x

    is_last = last_ref[s] == 1
    in_prompt = chunk_ref[s] < cp

    @pl.when(jnp.logical_and(is_last, in_prompt))
    def _():
        finish(xp_ref, op_ref)

    @pl.when(jnp.logical_and(is_last, jnp.logical_not(in_prompt)))
    def _():
        finish(xs_ref, os_ref)


def _combine(x_p, x_s, ys, slot0_rep, slot1_rep, g_final, sched, *, tm, tc, final_norm):
    d = x_p.shape[1]
    cp = x_p.shape[0] // tc
    chunk, tile, first, last, valid = sched
    tok = pl.BlockSpec((tc, LANES), lambda s, c, t, f, l, v: (c[s], 0))
    xp_spec, xs_spec = _group_specs(tc, d, cp, lambda pre: pre[0])
    grid_spec = pltpu.PrefetchScalarGridSpec(
        num_scalar_prefetch=5,
        grid=(chunk.shape[0],),
        in_specs=[tok, tok,
                  pl.BlockSpec((tm, d), lambda s, c, t, f, l, v: (t[s], 0)),
                  xp_spec, xs_spec,
                  pl.BlockSpec((1, d), lambda s, c, t, f, l, v: (0, 0))],
        out_specs=[xp_spec, xs_spec],
        scratch_shapes=[pltpu.VMEM((tc, d), F32)],
    )
    return pl.pallas_call(
        functools.partial(_combine_body, tm=tm, cp=cp, final_norm=final_norm),
        grid_spec=grid_spec,
        out_shape=[jax.ShapeDtypeStruct(x_p.shape, F32), jax.ShapeDtypeStruct(x_s.shape, F32)],
        compiler_params=_cparams(1),
        name="combine",
    )(chunk, tile, first, last, valid, slot0_rep, slot1_rep, ys, x_p, x_s, g_final.reshape(1, d))


def _flatten_ranges(lengths, n_steps):
    end = jnp.cumsum(lengths)
    total = end[-1]
    s = jnp.arange(n_steps, dtype=I32)
    sc = jnp.minimum(s, total - 1)
    seg = jnp.sum((end[None, :] <= sc[:, None]).astype(I32), axis=1)
    off = sc - (end - lengths)[seg]
    return seg, off, (s < total).astype(I32)


def _route(eidx, gates, *, tm, tc):
    n = eidx.shape[0]
    n_pairs = n * TOP_K
    n_chunks = n // tc
    n_tiles = n_pairs // tm + N_EXPERTS
    n_steps = n_tiles + N_EXPERTS * n_chunks
    experts = jnp.arange(N_EXPERTS, dtype=I32)

    flat_e = eidx.reshape(-1)
    onehot = (flat_e[:, None] == experts[None, :]).astype(I32)
    csum = jnp.cumsum(onehot, axis=0)
    rank = jnp.sum(csum * onehot, axis=1) - 1
    counts = csum[-1]
    tiles_per = (counts + tm - 1) // tm
    tile_end = jnp.cumsum(tiles_per)
    tile_start = tile_end - tiles_per
    offs = tile_start * tm
    slot = jnp.sum(onehot * offs[None, :], axis=1) + rank
    n_valid = tile_end[-1:]

    upd = jnp.stack([(jnp.arange(n_pairs, dtype=I32) // TOP_K).astype(F32), gates.reshape(-1)], axis=1)
    init = jnp.concatenate([jnp.full((n_tiles * tm, 1), -1.0, F32), jnp.zeros((n_tiles * tm, 1), F32)], axis=1)
    per_slot = init.at[slot].set(upd)
    src_rep = jnp.broadcast_to(per_slot[:, 0:1].astype(I32), (n_tiles * tm, LANES))
    gate_rep = jnp.broadcast_to(per_slot[:, 1:2], (n_tiles * tm, LANES))

    t = jnp.arange(n_tiles, dtype=I32)
    used = t < n_valid[0]
    te = jnp.minimum(jnp.sum((tile_end[None, :] <= t[:, None]).astype(I32), axis=1), N_EXPERTS - 1)
    tile_expert = jnp.where(used, te, te[n_valid[0] - 1])
    r_first = (t - tile_start[te]) * tm
    r_last = jnp.minimum(r_first + tm, counts[te]) - 1
    cs_t = csum[:, te]
    p_first = jnp.sum((cs_t <= r_first[None, :]).astype(I32), axis=0)
    p_last = jnp.sum((cs_t <= r_last[None, :]).astype(I32), axis=0)
    c_lo = (p_first // TOP_K) // tc
    c_hi = (p_last // TOP_K) // tc
    d_tile, d_off, d_valid = _flatten_ranges(jnp.where(used, c_hi - c_lo + 1, 1), n_steps)
    d_chunk = jnp.where(used[d_tile], c_lo[d_tile] + d_off, 0)
    d_first = d_valid * (d_off == 0).astype(I32)
    dispatch_sched = (d_tile, d_chunk, d_first, d_valid)

    before = jnp.concatenate([jnp.zeros((1, N_EXPERTS), I32), csum[TOP_K * tc - 1::TOP_K * tc]], axis=0)
    cnt = before[1:] - before[:-1]
    lo_slot = offs[None, :] + before[:-1]
    lo_tile = (lo_slot // tm).reshape(-1)
    hi_tile = ((lo_slot + cnt - 1) // tm).reshape(-1)
    span = jnp.where(cnt.reshape(-1) > 0, hi_tile - lo_tile + 1, 0)
    c_seg, c_off, c_valid = _flatten_ranges(span, n_steps)
    c_chunk = c_seg // N_EXPERTS
    c_tile = lo_tile[c_seg] + c_off
    prev_chunk = jnp.concatenate([jnp.full((1,), -1, I32), c_chunk[:-1]])
    next_chunk = jnp.concatenate([c_chunk[1:], jnp.full((1,), -1, I32)])
    next_valid = jnp.concatenate([c_valid[1:], jnp.zeros((1,), I32)])
    c_first = c_valid * (c_chunk != prev_chunk).astype(I32)
    c_last = c_valid * jnp.logical_or(c_chunk != next_chunk, next_valid == 0).astype(I32)
    combine_sched = (c_chunk, c_tile, c_first, c_last, c_valid)

    slot2 = slot.reshape(n, TOP_K)
    slot0_rep = jnp.broadcast_to(slot2[:, 0:1], (n, LANES))
    slot1_rep = jnp.broadcast_to(slot2[:, 1:2], (n, LANES))
    return src_rep, gate_rep, tile_expert, n_valid, dispatch_sched, slot0_rep, slot1_rep, combine_sched


def kernel(x_prompt, x_sample, cache_k, cache_v, page_table, norm_mix, w_in, lam_q1, lam_k1, lam_q2, lam_k2,
           attn_head_gain, sgu_ln_g, sgu_ln_b, sgu_w, sgu_b, w_out, norm_ffn, ffn_w_gate, ffn_w_up,
           ffn_w_down, moe_router, moe_w_gate, moe_w_up, moe_w_down, norm_final):
    depth = w_in.shape[0]
    b, t, d = x_prompt.shape
    n_s, s_len, _ = x_sample.shape
    n_p = b * t
    n_sr = n_s * s_len
    n_pool = cache_k.shape[1]
    assert depth % 2 == 0, "the last layer is expected to be a routed layer"
    assert CHUNK % s_len == 0

    n_pg = t // PAGE_SIZE
    cache_kt = jnp.swapaxes(cache_k.reshape(depth, n_pool, PAGE_SIZE, d), 2, 3)
    cache_vr = cache_v.reshape(depth, n_pool, PAGE_SIZE * N_HEADS, V_DIM)
    xp = x_prompt.reshape(n_p, d)
    xs = x_sample.reshape(n_sr, d)
    tm_p, tm_s = 512, 512

    kv_stacks = None
    k_s, v_s, sv_s = [], [], []
    for l in range(depth):
        lam_init = 0.8 - 0.6 * math.exp(-0.3 * l)
        lamp = jnp.stack([lam_q1[l], lam_k1[l], lam_q2[l], lam_k2[l]])
        w_in_bf = w_in[l].astype(BF16)
        w_out_bf = w_out[l].astype(BF16)
        gain = attn_head_gain[l]
        ws_p = sgu_w[l]
        bt_p = sgu_b[l].T
        reps = CHUNK // s_len
        eye = jnp.eye(reps, dtype=F32)
        ws_s = jax.vmap(lambda w: jnp.kron(eye, w))(sgu_w[l][:, :s_len, :s_len])
        bt_s = jnp.tile(sgu_b[l][:, :s_len].T, (reps, 1))
        routed = l % 2 == 1
        j = l // 2
        router_pad = None
        if routed:
            r_t = moe_router[j].T
            r_hi = r_t.astype(BF16)
            r_lo = (r_t - r_hi.astype(F32)).astype(BF16)
            router_pad = jnp.stack([jnp.concatenate([r_hi, r_lo]), jnp.concatenate([r_hi, jnp.zeros_like(r_hi)])])

        q, k, v, u, vn, sga, sgs, v_bf = _proj(xp, norm_mix[l], w_in_bf, sgu_ln_g[l], sgu_ln_b[l],
                                               tm=tm_p, paged=True, layer=l, depth=depth, stacks=kv_stacks)
        kv_stacks = (k, v)
        q2, k2, v2, u2, vn2, sga2, sgs2, vnf = _proj(xs, norm_mix[l], w_in_bf, sgu_ln_g[l], sgu_ln_b[l],
                                                     tm=tm_s, paged=False)
        ag, ag2 = _attn(q.reshape(b, t, d), k.reshape(depth, b, n_pg, d, PAGE_SIZE), v_bf.reshape(b, t, d),
                        sga.reshape(b, t, d), q2.reshape(n_s, s_len, d), k2.reshape(n_s, s_len, d),
                        v2.reshape(n_s, s_len, d), sga2.reshape(n_s, s_len, d), cache_kt, cache_vr, page_table,
                        gain, lamp, layer=l, lam_init=lam_init, tq=512)
        mix_p = _mix_out(ag.reshape(n_p, d), u, vn, sgs, xp, ws_p, bt_p, w_out_bf, norm_ffn[l], router_pad, tm=tm_p)
        mix_s = _mix_out(ag2.reshape(n_sr, d), u2, vn2, sgs2, xs, ws_s, bt_s, w_out_bf, norm_ffn[l], router_pad,
                         tm=tm_s)
        k_s.append(k2)
        v_s.append(v2)
        sv_s.append(vnf)

        if not routed:
            wg, wu, wd = (ffn_w_gate[j:j + 1].astype(BF16), ffn_w_up[j:j + 1].astype(BF16),
                          ffn_w_down[j:j + 1].astype(BF16))
            tf = wg.shape[2] // 2
            outs = []
            for (x_new, h), tm in ((mix_p, tm_p), (mix_s, tm_s)):
                nt = x_new.shape[0] // tm
                outs.append(_ffn(h, wg, wu, wd, jnp.zeros((nt,), I32), jnp.full((1,), nt, I32),
                                 x_new, tm=tm, tf=tf, routed=False))
            xp, xs = outs
        else:
            tm = tc = 512
            assert n_p % tc == 0 and n_sr % tc == 0
            eidx =jnp.concatenate([mix_p[2][:TOP_K], mix_s[2][:TOP_K]], axis=1).T
            gates = jnp.concatenate([mix_p[3][:TOP_K], mix_s[3][:TOP_K]], axis=1).T
            (src_rep, gate_rep, tile_expert, n_valid, dispatch_sched,
             slot0_rep, slot1_rep, combine_sched) = _route(eidx, gates, tm=tm, tc=tc)
            hs = _dispatch(mix_p[1], mix_s[1], src_rep, dispatch_sched, tm=tm, tc=tc)
            ys = _ffn(hs, moe_w_gate[j].astype(BF16), moe_w_up[j].astype(BF16), moe_w_down[j].astype(BF16),
                      tile_expert, n_valid, gate_rep, tm=tm, tf=moe_w_gate.shape[3] // 2, routed=True)
            xp, xs = _combine(mix_p[0], mix_s[0], ys, slot0_rep, slot1_rep, norm_final, combine_sched,
                              tm=tm, tc=tc, final_norm=(l == depth - 1))

    y_prompt = xp.reshape(b, t, d)
    y_sample = xs.reshape(n_s, s_len, d)
    k_prompt = jnp.moveaxis(kv_stacks[0].reshape(depth, b, n_pg, N_HEADS, 2, HEAD_DIM, PAGE_SIZE), 6, 3)
    v_prompt = kv_stacks[1].reshape(depth, b, n_pg, PAGE_SIZE, N_HEADS, V_DIM)
    k_sample = jnp.stack(k_s).reshape(depth, n_s, s_len, N_HEADS, 2, HEAD_DIM)
    v_sample = jnp.stack(v_s).reshape(depth, n_s, s_len, N_HEADS, V_DIM)
    sgu_v_sample = jnp.stack(sv_s).reshape(depth, n_s, s_len, d)
    return (y_prompt, y_sample, k_prompt, v_prompt, k_sample, v_sample, sgu_v_sample)
```

```python
import functools
import math

import jax
import jax.numpy as jnp
from jax import lax
from jax.experimental import pallas as pl
from jax.experimental.pallas import tpu as pltpu

F32 = jnp.float32
BF16 = jnp.bfloat16
I32 = jnp.int32

N_HEADS = 8
HEAD_DIM = 64
V_DIM = 2 * HEAD_DIM
PAIR = 2 * V_DIM
CHUNK = 128
N_GROUPS = 8
GROUP_DIM = 128
N_EXPERTS = 8
TOP_K = 2
PAGE_SIZE = 128
EPS = 1e-6
NEG = -1e30
Q_PRESCALE = HEAD_DIM ** -0.5 * math.log2(math.e)
N_SEG = 7
LANES = 128
VMEM_LIMIT = 56 * 1024 * 1024

NT_DIMS = (((1,), (1,)), ((), ()))


def _cparams(n_axes, vmem=VMEM_LIMIT):
    return pltpu.CompilerParams(dimension_semantics=("arbitrary",) * n_axes,
                                vmem_limit_bytes=vmem)


def _rms(x, gain):
    return x * lax.rsqrt(jnp.mean(x * x, axis=-1, keepdims=True) + EPS) * gain


def _lam_value(lamp_ref, lam_init):
    a = jnp.sum(lamp_ref[0:1, :] * lamp_ref[1:2, :], axis=-1, keepdims=True)
    b = jnp.sum(lamp_ref[2:3, :] * lamp_ref[3:4, :], axis=-1, keepdims=True)
    return jnp.exp(a) - jnp.exp(b) + lam_init


def _one_hot_cols(rel, width):
    lane = lax.broadcasted_iota(I32, rel.shape, 1)
    return jnp.concatenate([jnp.where(rel == lane + q * LANES, 1.0, 0.0) for q in range(width // LANES)], axis=1)


def _proj_body(x_ref, g_ref, w_ref, lng_ref, lnb_ref, *rest, paged, first_layer):
    q_ref, k_ref, v_ref, u_ref, vn_ref, sga_ref, sgs_ref, extra_ref = rest[-8:]
    tm, d = x_ref.shape
    h = _rms(x_ref[...], g_ref[...]).astype(BF16)

    def seg(s):
        return jnp.dot(h, w_ref[:, s * d:(s + 1) * d], preferred_element_type=F32)

    q_ref[...] = (seg(0) * Q_PRESCALE).astype(BF16)
    if paged:
        if first_layer:
            if k_ref.shape[0] > 1:
                k_ref[1:] = jnp.zeros((k_ref.shape[0] - 1,) + k_ref.shape[1:], F32)
                v_ref[1:] = jnp.zeros((v_ref.shape[0] - 1,) + v_ref.shape[1:], F32)
            k_ref, v_ref = k_ref.at[0], v_ref.at[0]
        kk = seg(1)
        for pg in range(tm // PAGE_SIZE):
            k_ref[pg] = kk[pg * PAGE_SIZE:(pg + 1) * PAGE_SIZE, :].T
        vv = seg(2)
        for hd in range(N_HEADS):
            v_ref[:, hd, :] = vv[:, hd * V_DIM:(hd + 1) * V_DIM]
        extra_ref[...] = vv.astype(BF16)
    else:
        k_ref[...] = seg(1)
        v_ref[...] = seg(2)
    u_ref[...] = jax.nn.gelu(seg(3)).astype(BF16)
    vg = jax.nn.gelu(seg(4))
    xc = vg - jnp.mean(vg, axis=-1, keepdims=True)
    vn = xc * lax.rsqrt(jnp.mean(xc * xc, axis=-1, keepdims=True) + EPS) * lng_ref[...] + lnb_ref[...]
    vn_ref[...] = vn.astype(BF16)
    if not paged:
        extra_ref[...] = vn
    sga_ref[...] = jax.nn.sigmoid(seg(5)).astype(BF16)
    sgs_ref[...] = jax.nn.sigmoid(seg(6)).astype(BF16)


def _proj(x, gain, w_bf, ln_g, ln_b, *, tm, paged, layer=0, depth=1, stacks=None):
    n, d = x.shape
    row = pl.BlockSpec((tm, d), lambda i: (i, 0))
    vec = pl.BlockSpec((1, d), lambda i: (0, 0))
    rows_bf = jax.ShapeDtypeStruct((n, d), BF16)
    rows_f32 = jax.ShapeDtypeStruct((n, d), F32)
    in_specs = [row, vec, pl.BlockSpec((d, N_SEG * d), lambda i: (0, 0), pipeline_mode=pl.Buffered(1)), vec, vec]
    args = [x, gain.reshape(1, d), w_bf, ln_g.reshape(1, d), ln_b.reshape(1, d)]
    aliases = {}
    if paged:
        first = stacks is None
        assert first == (layer == 0)
        slot = depth if first else None
        k_spec = pl.BlockSpec((slot, tm // PAGE_SIZE, d, PAGE_SIZE), lambda i: (0 if first else layer, i, 0, 0))
        k_shape = jax.ShapeDtypeStruct((depth, n // PAGE_SIZE, d, PAGE_SIZE), F32)
        v_spec = pl.BlockSpec((slot, tm, N_HEADS, V_DIM), lambda i: (0 if first else layer, i, 0, 0))
        v_shape = jax.ShapeDtypeStruct((depth, n, N_HEADS, V_DIM), F32)
        if not first:
            in_specs += [pl.BlockSpec(memory_space=pl.ANY)] * 2
            aliases = {len(args): 1, len(args) + 1: 2}
            args += list(stacks)
    else:
        k_spec, k_shape, v_spec, v_shape = row, rows_f32, row, rows_f32
    out_specs = [row, k_spec, v_spec, row, row, row, row, row]
    out_shape = [rows_bf, k_shape, v_shape, rows_bf, rows_bf, rows_bf, rows_bf, rows_bf if paged else rows_f32]
    return pl.pallas_call(
        functools.partial(_proj_body, paged=paged, first_layer=paged and stacks is None),
        grid=(n // tm,),
        in_specs=in_specs,
        out_specs=out_specs,
        out_shape=out_shape,
        input_output_aliases=aliases,
        compiler_params=_cparams(1),
        name="proj",
    )(*args)


def _prompt_attn_tile(i, q, sga, gain_ref, lamp_ref, kb, vb, qs, acc, m_s, l_s, *, tq, lam_init):
    lane = lax.broadcasted_iota(I32, q.shape, 1)
    zero = jnp.zeros_like(q)
    qs[0:tq, :] = jnp.where(lane < HEAD_DIM, q, zero)
    qs[tq:, :] = jnp.where(lane >= HEAD_DIM, q, zero)
    m_s[...] = jnp.full(m_s.shape, NEG, F32)
    l_s[...] = jnp.zeros(l_s.shape, F32)
    acc[...] = jnp.zeros(acc.shape, F32)

    def step(start, width, masked):
        s = jnp.dot(qs[...], kb[:, pl.ds(start, width)], preferred_element_type=F32)
        if masked:
            r = lax.broadcasted_iota(I32, s.shape, 0)
            r = jnp.where(r >= tq, r - tq, r)
            c = lax.broadcasted_iota(I32, s.shape, 1)
            s = jnp.where(c <= r, s, NEG)
        m_prev = m_s[...]
        m_new = jnp.maximum(m_prev, jnp.max(s, axis=-1, keepdims=True))
        alpha = jnp.exp2(m_prev - m_new)
        p = jnp.exp2(s - jnp.tile(m_new, (1, width // LANES)))
        l_s[...] = alpha * l_s[...] + jnp.sum(p, axis=-1, keepdims=True)
        acc[...] = alpha * acc[...] + jnp.dot(p.astype(BF16), vb[pl.ds(start, width), :],
                                              preferred_element_type=F32)
        m_s[...] = m_new

    def off_diag_pair(jj, carry):
        step(pl.multiple_of(jj * 2 * tq, 2 * tq), 2 * tq, False)
        return carry

    lax.fori_loop(0, i // 2, off_diag_pair, 0)

    @pl.when(i % 2 == 1)
    def _():
        step(pl.multiple_of((i - 1) * tq, tq), tq, False)

    step(pl.multiple_of(i * tq, tq), tq, True)

    o = acc[...] / l_s[...]
    lam = _lam_value(lamp_ref, lam_init)
    od = o[0:tq, :] - lam * o[tq:, :]
    y = _rms(od, gain_ref[...]) * (1.0 - lam_init)
    return (sga.astype(F32) * y).astype(BF16)


def _sample_attn_seq(slot, q_ref, kn_ref, vn_ref, sga_ref, gain_ref, lamp_ref, kbuf, vbuf, o_ref,
                     *, n_pages, s_len, lam_init):
    n_pairs = N_HEADS // 2
    prow = 4 * s_len
    r = lax.broadcasted_iota(I32, (prow, PAIR), 0)
    c = lax.broadcasted_iota(I32, (prow, PAIR), 1)
    on_diag = (c // HEAD_DIM) == (r // s_len)
    qp = []
    for hp in range(n_pairs):
        qt = jnp.tile(q_ref[:, hp * PAIR:(hp + 1) * PAIR], (4, 1))
        qp.append(jnp.where(on_diag, qt, jnp.zeros_like(qt)))

    def pair_rows(fn):
        return jnp.concatenate([fn(hp) for hp in range(n_pairs)], axis=0)

    s_past = [pair_rows(lambda hp: jnp.dot(qp[hp], kbuf[slot, p, hp * PAIR:(hp + 1) * PAIR, :].astype(BF16),
                                           preferred_element_type=F32))
              for p in range(n_pages)]
    s_new = pair_rows(lambda hp: lax.dot_general(qp[hp], kn_ref[:, hp * PAIR:(hp + 1) * PAIR].astype(BF16),
                                                 NT_DIMS, preferred_element_type=F32))
    rn = lax.broadcasted_iota(I32, s_new.shape, 0) % s_len
    cn = lax.broadcasted_iota(I32, s_new.shape, 1)
    s_new = jnp.where(cn <= rn, s_new, NEG)

    m = jnp.max(s_new, axis=-1, keepdims=True)
    for sp in s_past:
        m = jnp.maximum(m, jnp.max(sp, axis=-1, keepdims=True))
    p_new = jnp.exp2(s_new - m)
    l = jnp.sum(p_new, axis=-1, keepdims=True)
    p_new = p_new.astype(BF16)
    o = [jnp.dot(p_new[hp * prow:(hp + 1) * prow], vn_ref[:, hp * PAIR:(hp + 1) * PAIR].astype(BF16),
                 preferred_element_type=F32) for hp in range(n_pairs)]
    for p in range(n_pages):
        pp = jnp.exp2(s_past[p] - m)
        l = l + jnp.sum(pp, axis=-1, keepdims=True)
        pp = pp.astype(BF16)
        for hp in range(n_pairs):
            v_pair = jnp.concatenate([vbuf[slot, p, pl.ds(2 * hp + hl, PAGE_SIZE, stride=N_HEADS), :]
                                      for hl in range(2)], axis=1)
            o[hp] = o[hp] + jnp.dot(pp[hp * prow:(hp + 1) * prow], v_pair.astype(BF16),
                                    preferred_element_type=F32)

    lam = _lam_value(lamp_ref, lam_init)
    for h in range(N_HEADS):
        hp, hl = divmod(h, 2)
        cols = slice(h * V_DIM, (h + 1) * V_DIM)
        r0 = hl * 2 * s_len
        oh = o[hp][r0:r0 + 2 * s_len, hl * V_DIM:(hl + 1) * V_DIM] / l[h * 2 * s_len:(h + 1) * 2 * s_len]
        od = oh[0:s_len] - lam * oh[s_len:2 * s_len]
        y = _rms(od, gain_ref[h:h + 1, :]) * (1.0 - lam_init)
        o_ref[:, cols] = (sga_ref[:, cols].astype(F32) * y).astype(BF16)


def _attn_body(pt_ref, qp_ref, kt_ref, vb_ref, sgap_ref, gainp_ref, lamp_ref,
               qs_ref, kn_ref, vn_ref, sgas_ref, gains_ref, ck_ref, cv_ref, op_ref, os_ref,
               kb, qst, acc, m_s, l_s, kbuf, vbuf, sem,
               *, layer, n_pages, s_len, tq, tiles, steps_per_head, lam_init):
    n = pl.program_id(0)
    n_seq = pl.num_programs(0)

    def page_copies(seq, slot):
        out = []
        for p in range(n_pages):
            page = pt_ref[seq * n_pages + p]
            out.append(pltpu.make_async_copy(ck_ref.at[layer, page], kbuf.at[slot, p], sem.at[0, slot]))
            out.append(pltpu.make_async_copy(cv_ref.at[layer, page], vbuf.at[slot, p], sem.at[1, slot]))
        return out

    @pl.when(n == 0)
    def _():
        for cp in page_copies(0, 0):
            cp.start()

    @pl.when(n + 1 < n_seq)
    def _():
        for cp in page_copies(n + 1, (n + 1) % 2):
            cp.start()

    part = n % steps_per_head

    @pl.when(part == 0)
    def _():
        for pg in range(kt_ref.shape[0]):
            kb[:, pg * PAGE_SIZE:(pg + 1) * PAGE_SIZE] = kt_ref[pg].astype(BF16)

    for t in range(tiles):
        rows = slice(t * tq, (t + 1) * tq)
        op_ref[rows, :] = _prompt_attn_tile(part * tiles + t, qp_ref[rows, :], sgap_ref[rows, :], gainp_ref,
                                            lamp_ref, kb, vb_ref, qst, acc, m_s, l_s, tq=tq, lam_init=lam_init)

    slot = n % 2
    for cp in page_copies(n, slot):
        cp.wait()
    _sample_attn_seq(slot, qs_ref, kn_ref, vn_ref, sgas_ref, gains_ref, lamp_ref, kbuf, vbuf, os_ref,
                     n_pages=n_pages, s_len=s_len, lam_init=lam_init)


def _attn(q_p, kt_p, v_p, sga_p, q_s, k_new, v_new, sga_s, cache_kt, cache_v, page_table, gain, lamp,
          *, layer, lam_init, tq):
    b, t, d = q_p.shape
    n_seq, s_len, _ = q_s.shape
    n_pages = page_table.shape[1]
    assert V_DIM == LANES
    nq = t // tq
    tiles = (b * N_HEADS * nq) // n_seq
    assert tiles * n_seq == b * N_HEADS * nq and nq % tiles == 0
    steps_per_head = nq // tiles

    def bh(n):
        return n // (steps_per_head * N_HEADS), (n // steps_per_head) % N_HEADS

    qspec = pl.BlockSpec((None, tiles * tq, V_DIM), lambda n, pt: (bh(n)[0], n % steps_per_head, bh(n)[1]))
    kspec = pl.BlockSpec((None, None, t // PAGE_SIZE, V_DIM, PAGE_SIZE),
                         lambda n, pt: (layer, bh(n)[0], 0, bh(n)[1], 0))
    vspec = pl.BlockSpec((None, t, V_DIM), lambda n, pt: (bh(n)[0], 0, bh(n)[1]))
    row = pl.BlockSpec((None, s_len, d), lambda n, pt: (n, 0, 0))
    grid_spec = pltpu.PrefetchScalarGridSpec(
        num_scalar_prefetch=1,
        grid=(n_seq,),
        in_specs=[qspec, kspec, vspec, qspec,
                  pl.BlockSpec((None, 1, V_DIM), lambda n, pt: (bh(n)[1], 0, 0)),
                  pl.BlockSpec(lamp.shape, lambda n, pt: (0, 0)),
                  row, row, row, row,
                  pl.BlockSpec(gain.shape, lambda n, pt: (0, 0)),
                  pl.BlockSpec(memory_space=pl.ANY), pl.BlockSpec(memory_space=pl.ANY)],
        out_specs=[qspec, row],
        scratch_shapes=[pltpu.VMEM((V_DIM, t), BF16),
                        pltpu.VMEM((2 * tq, V_DIM), BF16), pltpu.VMEM((2 * tq, V_DIM), F32),
                        pltpu.VMEM((2 * tq, LANES), F32), pltpu.VMEM((2 * tq, LANES), F32),
                        pltpu.VMEM((2, n_pages, d, PAGE_SIZE), F32),
                        pltpu.VMEM((2, n_pages, PAGE_SIZE * N_HEADS, V_DIM), F32),
                        pltpu.SemaphoreType.DMA((2, 2))],
    )
    return pl.pallas_call(
        functools.partial(_attn_body, layer=layer, n_pages=n_pages, s_len=s_len, tq=tq, tiles=tiles,
                          steps_per_head=steps_per_head, lam_init=lam_init),
        grid_spec=grid_spec,
        out_shape=[jax.ShapeDtypeStruct((b, t, d), BF16), jax.ShapeDtypeStruct((n_seq, s_len, d), BF16)],
        compiler_params=_cparams(1),
        name="attn",
    )(page_table.reshape(-1), q_p, kt_p, v_p, sga_p, gain.reshape(N_HEADS, 1, V_DIM), lamp,
      q_s, k_new, v_new, sga_s, gain, cache_kt, cache_v)


def _mix_body(ag_ref, u_ref, vn_ref, sgs_ref, x_ref, ws_ref, bt_ref, wout_ref, gffn_ref, *rest,
              with_router):
    if with_router:
        router_ref, xo_ref, h_ref, eidx_ref, gate_ref, mixed = rest
    else:
        xo_ref, h_ref, mixed = rest
    tm = x_ref.shape[0]
    rr = lax.broadcasted_iota(I32, (CHUNK, CHUNK), 0)
    cc = lax.broadcasted_iota(I32, (CHUNK, CHUNK), 1)
    tril = cc <= rr
    for g in range(N_GROUPS):
        wg = jnp.where(tril, ws_ref[g], 0.0).astype(BF16)
        bias = bt_ref[:, g:g + 1]
        cols = slice(g * GROUP_DIM, (g + 1) * GROUP_DIM)
        for ci in range(tm // CHUNK):
            rows = slice(ci * CHUNK, (ci + 1) * CHUNK)
            s = jnp.dot(wg, vn_ref[rows, cols], preferred_element_type=F32) + bias
            sgu = u_ref[rows, cols].astype(F32) * s
            mix = ag_ref[rows, cols].astype(F32) + sgs_ref[rows, cols].astype(F32) * sgu
            mixed[rows, cols] = mix.astype(BF16)
    x_new = x_ref[...] + jnp.dot(mixed[...], wout_ref[...], preferred_element_type=F32)
    xo_ref[...] = x_new
    h = _rms(x_new, gffn_ref[...])
    h_ref[...] = h.astype(BF16)
    if with_router:
        h_hi = h.astype(BF16)
        h_lo = (h - h_hi.astype(F32)).astype(BF16)
        a = lax.dot_general(router_ref[0], h_hi, NT_DIMS, preferred_element_type=F32)
        b = lax.dot_general(router_ref[1], h_lo, NT_DIMS, preferred_element_type=F32)
        lg = a[0:N_EXPERTS] + a[N_EXPERTS:] + b[0:N_EXPERTS]
        ex = lax.broadcasted_iota(I32, lg.shape, 0)
        m1 = jnp.max(lg, axis=0, keepdims=True)
        i1 = jnp.min(jnp.where(lg == m1, ex, N_EXPERTS), axis=0, keepdims=True)
        lg2 = jnp.where(ex == i1, -jnp.inf, lg)
        m2 = jnp.max(lg2, axis=0, keepdims=True)
        i2 = jnp.min(jnp.where(lg2 == m2, ex, N_EXPERTS), axis=0, keepdims=True)
        e = jnp.exp(m2 - m1)
        g1 = 1.0 / (1.0 + e)
        g2 = e / (1.0 + e)
        eidx_ref[...] = jnp.where(ex == 0, i1, jnp.where(ex == 1, i2, 0))
        gate_ref[...] = jnp.where(ex == 0, g1, jnp.where(ex == 1, g2, 0.0))


def _mix_out(ag, u, vn, sgs, x, ws, bt, wout_bf, g_ffn, router_pad, *, tm):
    n, d = x.shape
    with_router = router_pad is not None
    row = pl.BlockSpec((tm, d), lambda i: (i, 0))
    per_expert = pl.BlockSpec((N_EXPERTS, tm), lambda i: (0, i))

    def full(a):
        return pl.BlockSpec(a.shape, lambda i: (0,) * a.ndim)

    g2 = g_ffn.reshape(1, d)
    in_specs = [row, row, row, row, row, full(ws), full(bt), full(wout_bf), full(g2)]
    args = [ag, u, vn, sgs, x, ws, bt, wout_bf, g2]
    out_specs = [row, row]
    out_shape = [jax.ShapeDtypeStruct((n, d), F32), jax.ShapeDtypeStruct((n, d), BF16)]
    if with_router:
        in_specs.append(full(router_pad))
        args.append(router_pad)
        out_specs += [per_expert, per_expert]
        out_shape += [jax.ShapeDtypeStruct((N_EXPERTS, n), I32), jax.ShapeDtypeStruct((N_EXPERTS, n), F32)]
    return pl.pallas_call(
        functools.partial(_mix_body, with_router=with_router),
        grid=(n // tm,),
        in_specs=in_specs,
        out_specs=out_specs,
        out_shape=out_shape,
        scratch_shapes=[pltpu.VMEM((tm, d), BF16)],
        compiler_params=_cparams(1),
        name="mix_out",
    )(*args)


def _ffn_body(te_ref, nv_ref, x_ref, wg_ref, wu_ref, wd_ref, side_ref, o_ref, acc, *, routed):
    del te_ref
    i = pl.program_id(0)
    j = pl.program_id(1)
    last = pl.num_programs(1) - 1
    valid = i < nv_ref[0]

    @pl.when(valid)
    def _():
        h = x_ref[...]
        a = jax.nn.silu(jnp.dot(h, wg_ref[...], preferred_element_type=F32))
        a = (a * jnp.dot(h, wu_ref[...], preferred_element_type=F32)).astype(BF16)
        y = jnp.dot(a, wd_ref[...], preferred_element_type=F32)

        @pl.when(j == 0)
        def _():
            acc[...] = y

        @pl.when(j > 0)
        def _():
            acc[...] += y

        @pl.when(j == last)
        def _():
            if routed:
                o_ref[...] = (acc[...] * jnp.tile(side_ref[...], (1, acc.shape[1] // LANES))).astype(o_ref.dtype)
            else:
                o_ref[...] = side_ref[...] + acc[...]

    @pl.when(jnp.logical_and(jnp.logical_not(valid), j == last))
    def _():
        o_ref[...] = jnp.zeros(o_ref.shape, o_ref.dtype)


def _ffn(x, wg, wu, wd, tile_expert, n_valid, side, *, tm, tf, routed):
    n, d = x.shape
    f = wg.shape[2]

    def row_idx(i, j, te, nv):
        return (jnp.minimum(i, nv[0] - 1), 0)

    side_spec = pl.BlockSpec((tm, side.shape[1]), row_idx)
    grid_spec = pltpu.PrefetchScalarGridSpec(
        num_scalar_prefetch=2,
        grid=(n // tm, f // tf),
        in_specs=[pl.BlockSpec((tm, d), row_idx),
                  pl.BlockSpec((None, d, tf), lambda i, j, te, nv: (te[i], 0, j)),
                  pl.BlockSpec((None, d, tf), lambda i, j, te, nv: (te[i], 0, j)),
                  pl.BlockSpec((None, tf, d), lambda i, j, te, nv: (te[i], j, 0)),
                  side_spec],
        out_specs=pl.BlockSpec((tm, d), lambda i, j, te, nv: (i, 0)),
        scratch_shapes=[pltpu.VMEM((tm, d), F32)],
    )
    return pl.pallas_call(
        functools.partial(_ffn_body, routed=routed),
        grid_spec=grid_spec,
        out_shape=jax.ShapeDtypeStruct((n, d), BF16 if routed else F32),
        compiler_params=_cparams(2),
        name="ffn",
    )(tile_expert, n_valid, x, wg, wu, wd, side)


def _dispatch_body(tile_ref, chunk_ref, first_ref, valid_ref, src_ref, hp_ref, hs_ref, o_ref, *, tc, cp):
    del tile_ref
    s = pl.program_id(0)

    @pl.when(valid_ref[s] == 1)
    def _():
        chunk = chunk_ref[s]
        rel = src_ref[...] - chunk * tc
        pick = _one_hot_cols(rel, tc).astype(BF16)
        h = jnp.where(chunk < cp, hp_ref[...], hs_ref[...])
        rows = jnp.dot(pick, h, preferred_element_type=F32).astype(BF16)

        @pl.when(first_ref[s] == 1)
        def _():
            o_ref[...] = jnp.zeros(o_ref.shape, BF16)

        o_ref[...] += rows


def _group_specs(tc, width, cp, chunk_of):
    return (pl.BlockSpec((tc, width), lambda s, *pre: (jnp.minimum(chunk_of(pre)[s], cp - 1), 0)),
            pl.BlockSpec((tc, width), lambda s, *pre: (jnp.maximum(chunk_of(pre)[s] - cp, 0), 0)))


def _dispatch(h_p, h_s, src_rep, sched, *, tm, tc):
    d = h_p.shape[1]
    cp = h_p.shape[0] // tc
    n_slots = src_rep.shape[0]
    tile, chunk, first, valid = sched
    hp_spec, hs_spec = _group_specs(tc, d, cp, lambda pre: pre[1])
    grid_spec = pltpu.PrefetchScalarGridSpec(
        num_scalar_prefetch=4,
        grid=(tile.shape[0],),
        in_specs=[pl.BlockSpec((tm, LANES), lambda s, t, c, f, v: (t[s], 0)), hp_spec, hs_spec],
        out_specs=pl.BlockSpec((tm, d), lambda s, t, c, f, v: (t[s], 0)),
    )
    return pl.pallas_call(
        functools.partial(_dispatch_body, tc=tc, cp=cp),
        grid_spec=grid_spec,
        out_shape=jax.ShapeDtypeStruct((n_slots, d), BF16),
        compiler_params=_cparams(1),
        name="dispatch",
    )(tile, chunk, first, valid, src_rep, h_p, h_s)


def _combine_body(chunk_ref, tile_ref, first_ref, last_ref, valid_ref,
                  s0_ref, s1_ref, ys_ref, xp_ref, xs_ref, gfin_ref, op_ref, os_ref, acc, *, tm, cp, final_norm):
    s = pl.program_id(0)

    @pl.when(first_ref[s] == 1)
    def _():
        acc[...] = jnp.zeros(acc.shape, F32)

    @pl.when(valid_ref[s] == 1)
    def _():
        base = tile_ref[s] * tm
        hit = _one_hot_cols(s0_ref[...] - base, tm) + _one_hot_cols(s1_ref[...] - base, tm)
        acc[...] += jnp.dot(hit.astype(BF16), ys_ref[...], preferred_element_type=F32)

    def finish(x_ref, o_ref):
        x = x_ref[...] + acc[...]
        o_ref[...] = _rms(x, gfin_ref[...]) if final_norm else x

    is_last = last_ref[s] == 1
    in_prompt = chunk_ref[s] < cp

    @pl.when(jnp.logical_and(is_last, in_prompt))
    def _():
        finish(xp_ref, op_ref)

    @pl.when(jnp.logical_and(is_last, jnp.logical_not(in_prompt)))
    def _():
        finish(xs_ref, os_ref)


def _combine(x_p, x_s, ys, slot0_rep, slot1_rep, g_final, sched, *, tm, tc, final_norm):
    d = x_p.shape[1]
    cp = x_p.shape[0] // tc
    chunk, tile, first, last, valid = sched
    tok = pl.BlockSpec((tc, LANES), lambda s, c, t, f, l, v: (c[s], 0))
    xp_spec, xs_spec = _group_specs(tc, d, cp, lambda pre: pre[0])
    grid_spec = pltpu.PrefetchScalarGridSpec(
        num_scalar_prefetch=5,
        grid=(chunk.shape[0],),
        in_specs=[tok, tok,
                  pl.BlockSpec((tm, d), lambda s, c, t, f, l, v: (t[s], 0)),
                  xp_spec, xs_spec,
                  pl.BlockSpec((1, d), lambda s, c, t, f, l, v: (0, 0))],
        out_specs=[xp_spec, xs_spec],
        scratch_shapes=[pltpu.VMEM((tc, d), F32)],
    )
    return pl.pallas_call(
        functools.partial(_combine_body, tm=tm, cp=cp, final_norm=final_norm),
        grid_spec=grid_spec,
        out_shape=[jax.ShapeDtypeStruct(x_p.shape, F32), jax.ShapeDtypeStruct(x_s.shape, F32)],
        compiler_params=_cparams(1),
        name="combine",
    )(chunk, tile, first, last, valid, slot0_rep, slot1_rep, ys, x_p, x_s, g_final.reshape(1, d))


def _flatten_ranges(lengths, n_steps):
    end = jnp.cumsum(lengths)
    total = end[-1]
    s = jnp.arange(n_steps, dtype=I32)
    sc = jnp.minimum(s, total - 1)
    seg = jnp.sum((end[None, :] <= sc[:, None]).astype(I32), axis=1)
    off = sc - (end - lengths)[seg]
    return seg, off, (s < total).astype(I32)


def _route(eidx, gates, *, tm, tc):
    n = eidx.shape[0]
    n_pairs = n * TOP_K
    n_chunks = n // tc
    n_tiles = n_pairs // tm + N_EXPERTS
    n_steps = n_tiles + N_EXPERTS * n_chunks
    experts = jnp.arange(N_EXPERTS, dtype=I32)

    flat_e = eidx.reshape(-1)
    onehot = (flat_e[:, None] == experts[None, :]).astype(I32)
    csum = jnp.cumsum(onehot, axis=0)
    rank = jnp.sum(csum * onehot, axis=1) - 1
    counts = csum[-1]
    tiles_per = (counts + tm - 1) // tm
    tile_end = jnp.cumsum(tiles_per)
    tile_start = tile_end - tiles_per
    offs = tile_start * tm
    slot = jnp.sum(onehot * offs[None, :], axis=1) + rank
    n_valid = tile_end[-1:]

    upd = jnp.stack([(jnp.arange(n_pairs, dtype=I32) // TOP_K).astype(F32), gates.reshape(-1)], axis=1)
    init = jnp.concatenate([jnp.full((n_tiles * tm, 1), -1.0, F32), jnp.zeros((n_tiles * tm, 1), F32)], axis=1)
    per_slot = init.at[slot].set(upd)
    src_rep = jnp.broadcast_to(per_slot[:, 0:1].astype(I32), (n_tiles * tm, LANES))
    gate_rep = jnp.broadcast_to(per_slot[:, 1:2], (n_tiles * tm, LANES))

    t = jnp.arange(n_tiles, dtype=I32)
    used = t < n_valid[0]
    te = jnp.minimum(jnp.sum((tile_end[None, :] <= t[:, None]).astype(I32), axis=1), N_EXPERTS - 1)
    tile_expert = jnp.where(used, te, te[n_valid[0] - 1])
    r_first = (t - tile_start[te]) * tm
    r_last = jnp.minimum(r_first + tm, counts[te]) - 1
    cs_t = csum[:, te]
    p_first = jnp.sum((cs_t <= r_first[None, :]).astype(I32), axis=0)
    p_last = jnp.sum((cs_t <= r_last[None, :]).astype(I32), axis=0)
    c_lo = (p_first // TOP_K) // tc
    c_hi = (p_last // TOP_K) // tc
    d_tile, d_off, d_valid = _flatten_ranges(jnp.where(used, c_hi - c_lo + 1, 1), n_steps)
    d_chunk = jnp.where(used[d_tile], c_lo[d_tile] + d_off, 0)
    d_first = d_valid * (d_off == 0).astype(I32)
    dispatch_sched = (d_tile, d_chunk, d_first, d_valid)

    before = jnp.concatenate([jnp.zeros((1, N_EXPERTS), I32), csum[TOP_K * tc - 1::TOP_K * tc]], axis=0)
    cnt = before[1:] - before[:-1]
    lo_slot = offs[None, :] + before[:-1]
    lo_tile = (lo_slot // tm).reshape(-1)
    hi_tile = ((lo_slot + cnt - 1) // tm).reshape(-1)
    span = jnp.where(cnt.reshape(-1) > 0, hi_tile - lo_tile + 1, 0)
    c_seg, c_off, c_valid = _flatten_ranges(span, n_steps)
    c_chunk = c_seg // N_EXPERTS
    c_tile = lo_tile[c_seg] + c_off
    prev_chunk = jnp.concatenate([jnp.full((1,), -1, I32), c_chunk[:-1]])
    next_chunk = jnp.concatenate([c_chunk[1:], jnp.full((1,), -1, I32)])
    next_valid = jnp.concatenate([c_valid[1:], jnp.zeros((1,), I32)])
    c_first = c_valid * (c_chunk != prev_chunk).astype(I32)
    c_last = c_valid * jnp.logical_or(c_chunk != next_chunk, next_valid == 0).astype(I32)
    combine_sched = (c_chunk, c_tile, c_first, c_last, c_valid)

    slot2 = slot.reshape(n, TOP_K)
    slot0_rep = jnp.broadcast_to(slot2[:, 0:1], (n, LANES))
    slot1_rep = jnp.broadcast_to(slot2[:, 1:2], (n, LANES))
    return src_rep, gate_rep, tile_expert, n_valid, dispatch_sched, slot0_rep, slot1_rep, combine_sched


def kernel(x_prompt, x_sample, cache_k, cache_v, page_table, norm_mix, w_in, lam_q1, lam_k1, lam_q2, lam_k2,
           attn_head_gain, sgu_ln_g, sgu_ln_b, sgu_w, sgu_b, w_out, norm_ffn, ffn_w_gate, ffn_w_up,
           ffn_w_down, moe_router, moe_w_gate, moe_w_up, moe_w_down, norm_final):
    depth = w_in.shape[0]
    b, t, d = x_prompt.shape
    n_s, s_len, _ = x_sample.shape
    n_p = b * t
    n_sr = n_s * s_len
    n_pool = cache_k.shape[1]
    assert depth % 2 == 0, "the last layer is expected to be a routed layer"
    assert CHUNK % s_len == 0

    n_pg = t // PAGE_SIZE
    cache_kt = jnp.swapaxes(cache_k.reshape(depth, n_pool, PAGE_SIZE, d), 2, 3)
    cache_vr = cache_v.reshape(depth, n_pool, PAGE_SIZE * N_HEADS, V_DIM)
    xp = x_prompt.reshape(n_p, d)
    xs = x_sample.reshape(n_sr, d)
    tm_p, tm_s = 512, 512

    kv_stacks = None
    k_s, v_s, sv_s = [], [], []
    for l in range(depth):
        lam_init = 0.8 - 0.6 * math.exp(-0.3 * l)
        lamp = jnp.stack([lam_q1[l], lam_k1[l], lam_q2[l], lam_k2[l]])
        w_in_bf = w_in[l].astype(BF16)
        w_out_bf = w_out[l].astype(BF16)
        gain = attn_head_gain[l]
        ws_p = sgu_w[l]
        bt_p = sgu_b[l].T
        reps = CHUNK // s_len
        eye = jnp.eye(reps, dtype=F32)
        ws_s = jax.vmap(lambda w: jnp.kron(eye, w))(sgu_w[l][:, :s_len, :s_len])
        bt_s = jnp.tile(sgu_b[l][:, :s_len].T, (reps, 1))
        routed = l % 2 == 1
        j = l // 2
        router_pad = None
        if routed:
            r_t = moe_router[j].T
            r_hi = r_t.astype(BF16)
            r_lo = (r_t - r_hi.astype(F32)).astype(BF16)
            router_pad = jnp.stack([jnp.concatenate([r_hi, r_lo]), jnp.concatenate([r_hi, jnp.zeros_like(r_hi)])])

        q, k, v, u, vn, sga, sgs, v_bf = _proj(xp, norm_mix[l], w_in_bf, sgu_ln_g[l], sgu_ln_b[l],
                                               tm=tm_p, paged=True, layer=l, depth=depth, stacks=kv_stacks)
        kv_stacks = (k, v)
        q2, k2, v2, u2, vn2, sga2, sgs2, vnf = _proj(xs, norm_mix[l], w_in_bf, sgu_ln_g[l], sgu_ln_b[l],
                                                     tm=tm_s, paged=False)
        ag, ag2 = _attn(q.reshape(b, t, d), k.reshape(depth, b, n_pg, d, PAGE_SIZE), v_bf.reshape(b, t, d),
                        sga.reshape(b, t, d), q2.reshape(n_s, s_len, d), k2.reshape(n_s, s_len, d),
                        v2.reshape(n_s, s_len, d), sga2.reshape(n_s, s_len, d), cache_kt, cache_vr, page_table,
                        gain, lamp, layer=l, lam_init=lam_init, tq=512)
        mix_p = _mix_out(ag.reshape(n_p, d), u, vn, sgs, xp, ws_p, bt_p, w_out_bf, norm_ffn[l], router_pad, tm=tm_p)
        mix_s = _mix_out(ag2.reshape(n_sr, d), u2, vn2, sgs2, xs, ws_s, bt_s, w_out_bf, norm_ffn[l], router_pad,
                         tm=tm_s)
        k_s.append(k2)
        v_s.append(v2)
        sv_s.append(vnf)

        if not routed:
            wg, wu, wd = (ffn_w_gate[j:j + 1].astype(BF16), ffn_w_up[j:j + 1].astype(BF16),
                          ffn_w_down[j:j + 1].astype(BF16))
            tf = wg.shape[2] // 2
            outs = []
            for (x_new, h), tm in ((mix_p, tm_p), (mix_s, tm_s)):
                nt = x_new.shape[0] // tm
                outs.append(_ffn(h, wg, wu, wd, jnp.zeros((nt,), I32), jnp.full((1,), nt, I32),
                                 x_new, tm=tm, tf=tf, routed=False))
            xp, xs = outs
        else:
            tm = tc = 512
            assert n_p % tc == 0 and n_sr % tc == 0
            eidx =jnp.concatenate([mix_p[2][:TOP_K], mix_s[2][:TOP_K]], axis=1).T
            gates = jnp.concatenate([mix_p[3][:TOP_K], mix_s[3][:TOP_K]], axis=1).T
            (src_rep, gate_rep, tile_expert, n_valid, dispatch_sched,
             slot0_rep, slot1_rep, combine_sched) = _route(eidx, gates, tm=tm, tc=tc)
            hs = _dispatch(mix_p[1], mix_s[1], src_rep, dispatch_sched, tm=tm, tc=tc)
            ys = _ffn(hs, moe_w_gate[j].astype(BF16), moe_w_up[j].astype(BF16), moe_w_down[j].astype(BF16),
                      tile_expert, n_valid, gate_rep, tm=tm, tf=moe_w_gate.shape[3] // 2, routed=True)
            xp, xs = _combine(mix_p[0], mix_s[0], ys, slot0_rep, slot1_rep, norm_final, combine_sched,
                              tm=tm, tc=tc, final_norm=(l == depth - 1))

    y_prompt = xp.reshape(b, t, d)
    y_sample = xs.reshape(n_s, s_len, d)
    k_prompt = jnp.moveaxis(kv_stacks[0].reshape(depth, b, n_pg, N_HEADS, 2, HEAD_DIM, PAGE_SIZE), 6, 3)
    v_prompt = kv_stacks[1].reshape(depth, b, n_pg, PAGE_SIZE, N_HEADS, V_DIM)
    k_sample = jnp.stack(k_s).reshape(depth, n_s, s_len, N_HEADS, 2, HEAD_DIM)
    v_sample = jnp.stack(v_s).reshape(depth, n_s, s_len, N_HEADS, V_DIM)
    sgu_v_sample = jnp.stack(sv_s).reshape(depth, n_s, s_len, d)
    return (y_prompt, y_sample, k_prompt, v_prompt, k_sample, v_sample, sgu_v_sample)
```
